```python
import math
import jax, jax.numpy as jnp
from jax import lax
import numpy as np


D_MODEL = 1024
BATCH = 32
SEQ = 2048
DEPTH = 4
DEC_BATCH = 1
DEC_SEQ = 16384
PAST_LEN = 128

N_MIXERS = 3
N_ATTN_LAYERS = (DEPTH + 2) // 3
N_POOL_LAYERS = (DEPTH + 1) // 3
N_MLSTM_LAYERS = DEPTH // 3

DA_HEADS = 8
DA_QK_DIM = 64
DA_V_DIM = 2 * DA_QK_DIM
ROPE_DIM = DA_QK_DIM // 4
ROPE_THETA = 500000.0
Q_BLOCK = 128

POOL_GROUPS = 4
POOL_WINDOWS = (2, 4, 8, 16)
POOL_GROUP_DIM = D_MODEL // POOL_GROUPS

ML_HEADS = 4
ML_QK_DIM = 128
ML_V_DIM = D_MODEL // ML_HEADS
ML_CHUNK = 128
ML_QK_COLS = ML_HEADS * ML_QK_DIM
ML_GATE_COLS = 4 * ML_HEADS
ML_IN_COLS = 2 * ML_QK_COLS + 2 * D_MODEL + ML_GATE_COLS

FFN_HIDDEN = ((8 * D_MODEL + 3 * 256 - 1) // (3 * 256)) * 256

NORM_EPS = 1e-6

kernel_name = "hybrid_diffattn_pool_mlstm_encoder"


def rms_norm(x, gain):
    xf = x.astype(jnp.float32)
    y = xf * lax.rsqrt(jnp.mean(xf * xf, axis=-1, keepdims=True) + NORM_EPS)
    return (y * gain.astype(jnp.float32)).astype(x.dtype)


def lambda_init_for(layer_idx):
    return 0.8 - 0.6 * math.exp(-0.3 * layer_idx)


def rope_tables(seq):
    inv = ROPE_THETA ** (-jnp.arange(0, ROPE_DIM, 2, dtype=jnp.float32) / ROPE_DIM)
    ang = jnp.arange(seq, dtype=jnp.float32)[:, None] * inv[None, :]
    return jnp.cos(ang), jnp.sin(ang)


def apply_partial_rope(x, cos, sin):
    rot = x[..., :ROPE_DIM].astype(jnp.float32)
    x1, x2 = rot[..., :ROPE_DIM // 2], rot[..., ROPE_DIM // 2:]
    c = cos[None, :, None, None, :]
    s = sin[None, :, None, None, :]
    rotated = jnp.concatenate([x1 * c - x2 * s, x2 * c + x1 * s], axis=-1).astype(x.dtype)
    return jnp.concatenate([rotated, x[..., ROPE_DIM:]], axis=-1)


def diff_attention(x, w_in, lam_params, subln_gain, w_out, lambda_init):
    B, S, _ = x.shape
    proj = x @ w_in
    q, k, v = jnp.split(proj, [D_MODEL, 2 * D_MODEL], axis=-1)
    q = q.reshape(B, S, DA_HEADS, 2, DA_QK_DIM)
    k = k.reshape(B, S, DA_HEADS, 2, DA_QK_DIM)
    v = v.reshape(B, S, DA_HEADS, DA_V_DIM)
    cos, sin = rope_tables(S)
    q = apply_partial_rope(q, cos, sin)
    k = apply_partial_rope(k, cos, sin)
    lp = lam_params.astype(jnp.float32)
    lam = jnp.exp(jnp.sum(lp[0] * lp[1])) - jnp.exp(jnp.sum(lp[2] * lp[3])) + lambda_init
    scale = DA_QK_DIM ** -0.5
    nb = S // Q_BLOCK
    q_blocks = jnp.moveaxis(q.reshape(B, nb, Q_BLOCK, DA_HEADS, 2, DA_QK_DIM), 1, 0)

    def attend(qb):
        s = jnp.einsum('bqhcd,bkhcd->cbhqk', qb, k).astype(jnp.float32) * scale
        p = jax.nn.softmax(s, axis=-1)
        a = p[0] - lam * p[1]
        return jnp.einsum('bhqk,bkhe->bqhe', a.astype(v.dtype), v)

    o = lax.map(attend, q_blocks)
    o = jnp.moveaxis(o, 0, 1).reshape(B, S, DA_HEADS, DA_V_DIM)
    o = rms_norm(o, subln_gain) * (1.0 - lambda_init)
    return o.reshape(B, S, D_MODEL) @ w_out


def pool_mixer(x, w_group, b_group, scale):
    B, S, _ = x.shape
    t = jnp.arange(S)
    xg = x.reshape(B, S, POOL_GROUPS, POOL_GROUP_DIM)
    outs = []
    for g, w in enumerate(POOL_WINDOWS):
        xs = xg[:, :, g, :].astype(jnp.float32)
        c = jnp.concatenate([jnp.zeros((B, 1, POOL_GROUP_DIM), jnp.float32),
                             jnp.cumsum(xs, axis=1)], axis=1)
        lo = jnp.clip(t - w // 2, 0, S)
        hi = jnp.clip(t + w // 2, 0, S)
        cnt = (hi - lo).astype(jnp.float32)
        mean = (jnp.take(c, hi, axis=1) - jnp.take(c, lo, axis=1)) / cnt[None, :, None]
        outs.append((mean - xs).astype(x.dtype))
    d = jnp.stack(outs, axis=2)
    y = jnp.einsum('bsgi,gio->bsgo', d, w_group).reshape(B, S, D_MODEL) + b_group
    return y * scale


def mlstm_chunk_scan(q, k, v, logi, logf):
    B, H, S, _ = q.shape
    nc = S // ML_CHUNK

    def to_chunks(a):
        return jnp.moveaxis(a.reshape((B, H, nc, ML_CHUNK) + a.shape[3:]), 2, 0)

    mask = jnp.tril(jnp.ones((ML_CHUNK, ML_CHUNK), dtype=bool))

    def step(carry, inp):
        C, n, m = carry
        qc, kc, vc, ic, fc = inp
        b = jnp.cumsum(fc, axis=-1)
        dlog = b[..., :, None] - b[..., None, :] + ic[..., None, :]
        dlog = jnp.where(mask, dlog, -jnp.inf)
        g = b + m[..., None]
        m_row = jnp.maximum(g, jnp.max(dlog, axis=-1))
        w_intra = jnp.exp(dlog - m_row[..., None])
        w_inter = jnp.exp(g - m_row)
        s = jnp.einsum('bhjd,bhsd->bhjs', qc, kc) * w_intra
        num = (w_inter[..., None] * jnp.einsum('bhjd,bhed->bhje', qc, C)
               + jnp.einsum('bhjs,bhse->bhje', s, vc))
        den = w_inter * jnp.einsum('bhjd,bhd->bhj', qc, n) + jnp.sum(s, axis=-1)
        h = num / jnp.maximum(jnp.abs(den), jnp.exp(-m_row))[..., None]
        b_last = b[..., -1]
        w_state_log = b_last[..., None] - b + ic
        m_new = jnp.maximum(b_last + m, jnp.max(w_state_log, axis=-1))
        w_state = jnp.exp(w_state_log - m_new[..., None])
        decay = jnp.exp(b_last + m - m_new)
        C_new = decay[..., None, None] * C + jnp.einsum('bhs,bhse,bhsd->bhed', w_state, vc, kc)
        n_new = decay[..., None] * n + jnp.einsum('bhs,bhsd->bhd', w_state, kc)
        return (C_new, n_new, m_new), h

    init = (jnp.zeros((B, H, v.shape[-1], q.shape[-1]), jnp.float32),
            jnp.zeros((B, H, q.shape[-1]), jnp.float32),
            jnp.zeros((B, H), jnp.float32))
    _, h = lax.scan(step, init, (to_chunks(q), to_chunks(k), to_chunks(v),
                                 to_chunks(logi), to_chunks(logf)))
    return jnp.moveaxis(h, 0, 2).reshape(B, H, S, v.shape[-1])


def mlstm_mixer(x, w_in, b_gate, head_gain, w_out):
    B, S, _ = x.shape
    proj = x @ w_in
    q, k, v, o, gates = jnp.split(
        proj, [ML_QK_COLS, 2 * ML_QK_COLS, 2 * ML_QK_COLS + D_MODEL,
               2 * ML_QK_COLS + 2 * D_MODEL], axis=-1)
    q = q.reshape(B, S, ML_HEADS, ML_QK_DIM).transpose(0, 2, 1, 3).astype(jnp.float32)
    k = (k.reshape(B, S, ML_HEADS, ML_QK_DIM).transpose(0, 2, 1, 3).astype(jnp.float32)
         * (ML_QK_DIM ** -0.5))
    v = v.reshape(B, S, ML_HEADS, ML_V_DIM).transpose(0, 2, 1, 3).astype(jnp.float32)
    gates = (gates.astype(jnp.float32) + b_gate.astype(jnp.float32)).reshape(B, S, 4, ML_HEADS)
    gates = jnp.transpose(gates, (2, 0, 3, 1))
    ig_f, fg_f, ig_b, fg_b = gates[0], gates[1], gates[2], gates[3]
    h_f = mlstm_chunk_scan(q, k, v, ig_f, jax.nn.log_sigmoid(fg_f))
    flip = lambda a: jnp.flip(a, axis=2)
    h_b = flip(mlstm_chunk_scan(flip(q), flip(k), flip(v), flip(ig_b),
                                flip(jax.nn.log_sigmoid(fg_b))))
    h = (h_f + h_b).transpose(0, 2, 1, 3)
    h = rms_norm(h, head_gain.reshape(ML_HEADS, ML_V_DIM))
    h = h.reshape(B, S, D_MODEL).astype(x.dtype) * jax.nn.sigmoid(o)
    return h @ w_out


def swiglu(x, w_gate_up, w_down):
    g, u = jnp.split(x @ w_gate_up, 2, axis=-1)
    return (jax.nn.silu(g) * u) @ w_down


def trunk(x, norm_gain, w_attn_in, attn_lambda, attn_subln_gain, w_attn_out,
          w_pool, b_pool, pool_scale, w_mlstm_in, b_mlstm_gate, mlstm_head_gain,
          w_mlstm_out, w_ffn_gate_up, w_ffn_down):
    for i in range(DEPTH):
        kind = i % N_MIXERS
        slot = i // N_MIXERS
        h = rms_norm(x, norm_gain[i, 0])
        if kind == 0:
            h = diff_attention(h, w_attn_in[slot], attn_lambda[slot], attn_subln_gain[slot],
                               w_attn_out[slot], lambda_init_for(i))
        elif kind == 1:
            h = pool_mixer(h, w_pool[slot], b_pool[slot], pool_scale[slot])
        else:
            h = mlstm_mixer(h, w_mlstm_in[slot], b_mlstm_gate[slot], mlstm_head_gain[slot],
                            w_mlstm_out[slot])
        x = x + rms_norm(h, norm_gain[i, 1])
        h = swiglu(rms_norm(x, norm_gain[i, 2]), w_ffn_gate_up[i], w_ffn_down[i])
        x = x + rms_norm(h, norm_gain[i, 3])
    return x


def setup_inputs(seed: int = 0) -> dict:
    key = jax.random.key(seed)
    ks = jax.random.split(key, 24)

    def normal(k, shape, scale):
        return jax.random.normal(k, shape, jnp.float32) * scale

    x_prompt = normal(ks[0], (BATCH, SEQ, D_MODEL), 1.0)
    x_sample = normal(ks[1], (DEC_BATCH, DEC_SEQ, D_MODEL), 1.0)
    norm_gain = 1.0 + normal(ks[2], (DEPTH, 4, D_MODEL), 0.02)
    w_attn_in = normal(ks[3], (N_ATTN_LAYERS, D_MODEL, 3 * D_MODEL), D_MODEL ** -0.5)
    attn_lambda = normal(ks[4], (N_ATTN_LAYERS, 4, DA_QK_DIM), 0.1)
    attn_subln_gain = 1.0 + normal(ks[5], (N_ATTN_LAYERS, DA_V_DIM), 0.02)
    w_attn_out = normal(ks[6], (N_ATTN_LAYERS, D_MODEL, D_MODEL), D_MODEL ** -0.5)
    w_pool = normal(ks[7], (N_POOL_LAYERS, POOL_GROUPS, POOL_GROUP_DIM, POOL_GROUP_DIM),
                    POOL_GROUP_DIM ** -0.5)
    b_pool = normal(ks[8], (N_POOL_LAYERS, D_MODEL), 0.02)
    pool_scale = 1.0 + normal(ks[9], (N_POOL_LAYERS, D_MODEL), 0.02)
    w_mlstm_in = normal(ks[10], (N_MLSTM_LAYERS, D_MODEL, ML_IN_COLS), D_MODEL ** -0.5)
    b_mlstm_gate = jnp.concatenate([
        normal(ks[11], (N_MLSTM_LAYERS, ML_HEADS), 0.1),
        3.0 + normal(ks[12], (N_MLSTM_LAYERS, ML_HEADS), 0.1),
        normal(ks[13], (N_MLSTM_LAYERS, ML_HEADS), 0.1),
        3.0 + normal(ks[14], (N_MLSTM_LAYERS, ML_HEADS), 0.1),
    ], axis=-1)
    mlstm_head_gain = 1.0 + normal(ks[15], (N_MLSTM_LAYERS, D_MODEL), 0.02)
    w_mlstm_out = normal(ks[16], (N_MLSTM_LAYERS, D_MODEL, D_MODEL), D_MODEL ** -0.5)
    w_ffn_gate_up = normal(ks[17], (DEPTH, D_MODEL, 2 * FFN_HIDDEN), D_MODEL ** -0.5)
    w_ffn_down = normal(ks[18], (DEPTH, FFN_HIDDEN, D_MODEL), FFN_HIDDEN ** -0.5)
    return {"x_prompt": x_prompt, "x_sample": x_sample, "norm_gain": norm_gain,
            "w_attn_in": w_attn_in, "attn_lambda": attn_lambda,
            "attn_subln_gain": attn_subln_gain, "w_attn_out": w_attn_out,
            "w_pool": w_pool, "b_pool": b_pool, "pool_scale": pool_scale,
            "w_mlstm_in": w_mlstm_in, "b_mlstm_gate": b_mlstm_gate,
            "mlstm_head_gain": mlstm_head_gain, "w_mlstm_out": w_mlstm_out,
            "w_ffn_gate_up": w_ffn_gate_up, "w_ffn_down": w_ffn_down}


def reference(x_prompt, x_sample, norm_gain, w_attn_in, attn_lambda, attn_subln_gain,
              w_attn_out, w_pool, b_pool, pool_scale, w_mlstm_in, b_mlstm_gate,
              mlstm_head_gain, w_mlstm_out, w_ffn_gate_up, w_ffn_down):
    y_prompt = trunk(x_prompt, norm_gain, w_attn_in, attn_lambda, attn_subln_gain, w_attn_out,
                     w_pool, b_pool, pool_scale, w_mlstm_in, b_mlstm_gate, mlstm_head_gain,
                     w_mlstm_out, w_ffn_gate_up, w_ffn_down)
    y_sample = trunk(x_sample, norm_gain, w_attn_in, attn_lambda, attn_subln_gain, w_attn_out,
                     w_pool, b_pool, pool_scale, w_mlstm_in, b_mlstm_gate, mlstm_head_gain,
                     w_mlstm_out, w_ffn_gate_up, w_ffn_down)
    return (y_prompt, y_sample)
```

```python
import functools
import math

import jax
import jax.numpy as jnp
from jax import lax
from jax.experimental import pallas as pl
from jax.experimental.pallas import tpu as pltpu

D_MODEL = 1024
DEPTH = 4
N_MIXERS = 3
DA_HEADS = 8
DA_QK_DIM = 64
DA_V_DIM = 128
ROPE_DIM = 16
ROPE_THETA = 500000.0
POOL_GROUPS = 4
POOL_WINDOWS = (2, 4, 8, 16)
POOL_GROUP_DIM = D_MODEL // POOL_GROUPS
POOL_HALO = 8
ML_HEADS = 4
ML_QK_DIM = 128
ML_V_DIM = 256
ML_CHUNK = 128
ML_QK_COLS = ML_HEADS * ML_QK_DIM
ML_GATE_COLS = 4 * ML_HEADS
FFN_HIDDEN = 2816
NORM_EPS = 1e-6
LOG2E = math.log2(math.e)

LANES = 128
VMEM_LIMIT = 56 * 1024 * 1024
TM_PROJ = 512
TM_FFN = 256
TQ_ATTN = 256
TM_ML = 512

BF16 = jnp.bfloat16
F32 = jnp.float32


def _params(*sem):
    return pltpu.CompilerParams(dimension_semantics=sem, vmem_limit_bytes=VMEM_LIMIT)


def _const_spec(shape):
    nd = len(shape)
    return pl.BlockSpec(shape, lambda *_: (0,) * nd, pipeline_mode=pl.Buffered(1))


def _rms(x, gain):
    ms = jnp.mean(x * x, axis=-1, keepdims=True)
    return x * lax.rsqrt(ms + NORM_EPS) * gain


def _dot(a, b):
    return jnp.dot(a, b, preferred_element_type=F32)


def _dot_nt(a, b):
    return lax.dot_general(a, b, (((1,), (1,)), ((), ())), preferred_element_type=F32)


def _attn_proj_kernel(x_ref, g_ref, wq_ref, wk_ref, wvt_ref, cos_ref, sin_ref,
                      q_ref, k_ref, vt_ref):
    xn = _rms(x_ref[...], g_ref[...]).astype(BF16)
    cos = cos_ref[...]
    sin = sin_ref[...]
    lane = lax.broadcasted_iota(jnp.int32, (1, LANES), 1) % DA_QK_DIM
    first_half = lane < ROPE_DIM // 2

    def rope(y, scale):
        outs = []
        for h in range(DA_HEADS):
            slab = y[:, h * LANES:(h + 1) * LANES]
            partner = jnp.where(first_half,
                                pltpu.roll(slab, LANES - ROPE_DIM // 2, 1),
                                pltpu.roll(slab, ROPE_DIM // 2, 1))
            outs.append(((slab * cos + partner * sin) * scale).astype(BF16))
        return jnp.concatenate(outs, axis=1)

    q_ref[...] = rope(_dot(xn, wq_ref[...]), DA_QK_DIM ** -0.5 * LOG2E)
    k_ref[...] = rope(_dot(xn, wk_ref[...]), 1.0)
    vt_ref[0] = _dot_nt(wvt_ref[...], xn).astype(BF16)


def _attn_proj(x2d, gain, wq, wk, wvt, cos_t, sin_t, seq):
    t = x2d.shape[0]
    tm = TM_PROJ
    nps = seq // tm
    row = pl.BlockSpec((tm, D_MODEL), lambda i: (i, 0))
    tab = pl.BlockSpec((tm, LANES), lambda i: (i % nps, 0))
    return pl.pallas_call(
        _attn_proj_kernel,
        grid=(t // tm,),
        in_specs=[row, _const_spec((1, D_MODEL)), _const_spec((D_MODEL, D_MODEL)),
                  _const_spec((D_MODEL, D_MODEL)), _const_spec((D_MODEL, D_MODEL)), tab, tab],
        out_specs=[row, row, pl.BlockSpec((1, D_MODEL, tm), lambda i: (i, 0, 0))],
        out_shape=[jax.ShapeDtypeStruct((t, D_MODEL), BF16),
                   jax.ShapeDtypeStruct((t, D_MODEL), BF16),
                   jax.ShapeDtypeStruct((t // tm, D_MODEL, tm), BF16)],
        compiler_params=_params("parallel"),
        name="attn_proj",
    )(x2d, gain, wq, wk, wvt, cos_t, sin_t)


def _diff_attn_kernel(q_ref, k_ref, vt_ref, lam_ref, gain_ref, o_ref, acc_ref, *,
                      n_kv, tk, lambda_init):
    tq = q_ref.shape[1]
    q = q_ref[0]
    lane = lax.broadcasted_iota(jnp.int32, (1, LANES), 1)
    zero = jnp.zeros_like(q)
    qm = (jnp.where(lane < DA_QK_DIM, q, zero), jnp.where(lane >= DA_QK_DIM, q, zero))
    acc_ref[...] = jnp.zeros_like(acc_ref)

    def body(j, carry):
        kc = k_ref[0, pl.ds(pl.multiple_of(j * tk, tk), tk), :]
        vt = vt_ref[0, j]
        new = []
        for c in range(2):
            m_old, l_old = carry[2 * c], carry[2 * c + 1]
            s = _dot_nt(kc, qm[c])
            m_new = jnp.maximum(m_old, jnp.max(s, axis=0, keepdims=True))
            alpha = jnp.exp2(m_old - m_new)
            p = jnp.exp2(s - m_new)
            l_new = alpha * l_old + jnp.sum(p, axis=0, keepdims=True)
            acc_ref[c] = alpha * acc_ref[c] + _dot(vt, p.astype(BF16))
            new += [m_new, l_new]
        return tuple(new)

    neg = jnp.full((1, tq), -jnp.inf, F32)
    zer = jnp.zeros((1, tq), F32)
    _, l1, _, l2 = lax.fori_loop(0, n_kv, body, (neg, zer, neg, zer))

    lp = lam_ref[...]
    lam = (jnp.exp(jnp.sum(lp[0:1] * lp[1:2], keepdims=True))
           - jnp.exp(jnp.sum(lp[2:3] * lp[3:4], keepdims=True)) + lambda_init)
    o = acc_ref[0] / l1 - lam * (acc_ref[1] / l2)
    ms = jnp.mean(o * o, axis=0, keepdims=True)
    y = o * lax.rsqrt(ms + NORM_EPS) * gain_ref[...] * (1.0 - lambda_init)
    o_ref[0] = y.T.astype(BF16)


def _diff_attn(q, k, vt, lam_params, gain_col, batch, seq, lambda_init):
    tq, tk = TQ_ATTN, TM_PROJ
    n_kv = seq // tk
    q3 = q.reshape(batch, seq, D_MODEL)
    k3 = k.reshape(batch, seq, D_MODEL)
    vt4 = vt.reshape(batch, n_kv, D_MODEL, tk)
    kern = functools.partial(_diff_attn_kernel, n_kv=n_kv, tk=tk, lambda_init=lambda_init)
    out = pl.pallas_call(
        kern,
        grid=(batch, DA_HEADS, seq // tq),
        in_specs=[pl.BlockSpec((1, tq, LANES), lambda b, h, i: (b, i, h)),
                  pl.BlockSpec((1, seq, LANES), lambda b, h, i: (b, 0, h)),
                  pl.BlockSpec((1, n_kv, DA_V_DIM, tk), lambda b, h, i: (b, 0, h, 0)),
                  pl.BlockSpec((4, DA_QK_DIM), lambda b, h, i: (0, 0)),
                  pl.BlockSpec((DA_V_DIM, 1), lambda b, h, i: (0, 0))],
        out_specs=pl.BlockSpec((1, tq, LANES), lambda b, h, i: (b, i, h)),
        out_shape=jax.ShapeDtypeStruct((batch, seq, D_MODEL), BF16),
        scratch_shapes=[pltpu.VMEM((2, DA_V_DIM, tq), F32)],
        compiler_params=_params("parallel", "parallel", "arbitrary"),
        name="diff_attn",
    )(q3, k3, vt4, lam_params, gain_col)
    return out.reshape(batch * seq, D_MODEL)


def _mlstm_proj_kernel(x_ref, g_ref, wq_ref, wkt_ref, wv_ref, wo_ref, wg_ref, wgt_ref,
                       bg_ref, bgt_ref, q_ref, kt_ref, v_ref, o_ref, gc_ref, gr_ref):
    xn = _rms(x_ref[...], g_ref[...]).astype(BF16)
    q_ref[...] = _dot(xn, wq_ref[...]).astype(BF16)
    kt = (_dot_nt(wkt_ref[...], xn) * ML_QK_DIM ** -0.5).astype(BF16)
    v_ref[...] = _dot(xn, wv_ref[...]).astype(BF16)
    o_ref[...] = _dot(xn, wo_ref[...])

    def gate_act(g, is_forget):
        log_sig = jnp.minimum(g, 0.0) - jnp.log1p(jnp.exp(-jnp.abs(g)))
        return jnp.where(is_forget, log_sig, g)

    gc = _dot(xn, wg_ref[...]) + bg_ref[...]
    col_kind = (lax.broadcasted_iota(jnp.int32, (1, ML_GATE_COLS), 1) // ML_HEADS) % 2
    gc_ref[...] = gate_act(gc, col_kind == 1)
    gr = _dot_nt(wgt_ref[...], xn) + bgt_ref[...]
    row_kind = (lax.broadcasted_iota(jnp.int32, (ML_GATE_COLS, 1), 0) // ML_HEADS) % 2
    gr = gate_act(gr, row_kind == 1)
    for c in range(x_ref.shape[0] // ML_CHUNK):
        kt_ref[0, c] = kt[:, c * ML_CHUNK:(c + 1) * ML_CHUNK]
        gr_ref[0, c] = gr[:, c * ML_CHUNK:(c + 1) * ML_CHUNK]


def _mlstm_proj(x2d, gain, wq, wkt, wv, wo, wg, wgt, bg, bgt):
    t = x2d.shape[0]
    tm = TM_PROJ
    cpt = tm // ML_CHUNK
    row = lambda n: pl.BlockSpec((tm, n), lambda i: (i, 0))
    return pl.pallas_call(
        _mlstm_proj_kernel,
        grid=(t // tm,),
        in_specs=[row(D_MODEL), _const_spec((1, D_MODEL)),
                  _const_spec((D_MODEL, ML_QK_COLS)), _const_spec((ML_QK_COLS, D_MODEL)),
                  _const_spec((D_MODEL, D_MODEL)), _const_spec((D_MODEL, D_MODEL)),
                  _const_spec((D_MODEL, ML_GATE_COLS)), _const_spec((ML_GATE_COLS, D_MODEL)),
                  _const_spec((1, ML_GATE_COLS)), _const_spec((ML_GATE_COLS, 1))],
        out_specs=[row(ML_QK_COLS),
                   pl.BlockSpec((1, cpt, ML_QK_COLS, ML_CHUNK), lambda i: (i, 0, 0, 0)),
                   row(D_MODEL), row(D_MODEL), row(ML_GATE_COLS),
                   pl.BlockSpec((1, cpt, ML_GATE_COLS, ML_CHUNK), lambda i: (i, 0, 0, 0))],
        out_shape=[jax.ShapeDtypeStruct((t, ML_QK_COLS), BF16),
                   jax.ShapeDtypeStruct((t // tm, cpt, ML_QK_COLS, ML_CHUNK), BF16),
                   jax.ShapeDtypeStruct((t, D_MODEL), BF16),
                   jax.ShapeDtypeStruct((t, D_MODEL), F32),
                   jax.ShapeDtypeStruct((t, ML_GATE_COLS), F32),
                   jax.ShapeDtypeStruct((t // tm, cpt, ML_GATE_COLS, ML_CHUNK), F32)],
        compiler_params=_params("parallel"),
        name="mlstm_proj",
    )(x2d, gain, wq, wkt, wv, wo, wg, wgt, bg, bgt)


def _mlstm_chunk(q, kt, v, gcol, grow, head, state_ref, m_ref, reverse):
    L = ML_CHUNK
    gi = (2 if reverse else 0) * ML_HEADS + head
    gf = gi + ML_HEADS
    i_col, f_col = gcol[:, gi:gi + 1], gcol[:, gf:gf + 1]
    i_row, f_row = grow[gi:gi + 1, :], grow[gf:gf + 1, :]
    r = lax.broadcasted_iota(jnp.int32, (L, L), 0)
    c = lax.broadcasted_iota(jnp.int32, (L, L), 1)
    keep = (c >= r) if reverse else (c <= r)
    zero = jnp.zeros((L, L), F32)
    b_col = jnp.sum(jnp.where(keep, f_row, zero), axis=1, keepdims=True)
    b_row = jnp.sum(jnp.where(keep, zero, f_col) + jnp.where(r == c, f_col, zero),
                    axis=0, keepdims=True)
    b_last = jnp.sum(f_row, axis=1, keepdims=True)
    m_old = m_ref[0:1, 0:1]

    dlog = jnp.where(keep, b_col - b_row + i_row, -jnp.inf)
    g = b_col + m_old
    m_row = jnp.maximum(g, jnp.max(dlog, axis=1, keepdims=True))
    w_intra = jnp.exp(dlog - m_row)
    w_inter = jnp.exp(g - m_row)

    s = _dot(q, kt) * w_intra
    state = state_ref[...]
    inter = _dot(q, state.astype(BF16))
    num = w_inter * inter[:, :ML_V_DIM] + _dot(s.astype(BF16), v)
    den = w_inter * inter[:, ML_V_DIM:ML_V_DIM + 1] + jnp.sum(s, axis=1, keepdims=True)
    h = num / jnp.maximum(jnp.abs(den), jnp.exp(-m_row))

    w_state_log = b_last - b_col + i_col
    m_new = jnp.maximum(b_last + m_old, jnp.max(w_state_log, axis=0, keepdims=True))
    w_state = jnp.exp(w_state_log - m_new)
    decay = jnp.exp(b_last + m_old - m_new)
    ones_col = (lax.broadcasted_iota(jnp.int32, (L, LANES), 1) == 0).astype(F32)
    v_aug = jnp.concatenate([v.astype(F32), ones_col], axis=1)
    state_ref[...] = decay * state + _dot(kt, (w_state * v_aug).astype(BF16))
    m_ref[...] = jnp.broadcast_to(m_new, m_ref.shape)
    return h


def _mlstm_scan_kernel(qf_ref, ktf_ref, vf_ref, gcf_ref, grf_ref,
                       qb_ref, ktb_ref, vb_ref, gcb_ref, grb_ref,
                       hf_ref, hb_ref, state_ref, m_ref):
    @pl.when(pl.program_id(1) == 0)
    def _():
        state_ref[...] = jnp.zeros_like(state_ref)
        m_ref[...] = jnp.zeros_like(m_ref)

    n_chunks = qf_ref.shape[1] // ML_CHUNK
    fwd_refs = (qf_ref, ktf_ref, vf_ref, gcf_ref, grf_ref, hf_ref)
    bwd_refs = (qb_ref, ktb_ref, vb_ref, gcb_ref, grb_ref, hb_ref)

    def step(i, carry):
        for head in range(ML_HEADS):
            for reverse in (False, True):
                q_ref, kt_ref, v_ref, gc_ref, gr_ref, h_ref = bwd_refs if reverse else fwd_refs
                c = n_chunks - 1 - i if reverse else i
                rows = pl.ds(pl.multiple_of(c * ML_CHUNK, ML_CHUNK), ML_CHUNK)
                qk_cols = slice(head * ML_QK_DIM, (head + 1) * ML_QK_DIM)
                v_cols = slice(head * ML_V_DIM, (head + 1) * ML_V_DIM)
                slot = head * 2 + int(reverse)
                h_ref[0, rows, v_cols] = _mlstm_chunk(
                    q_ref[0, rows, qk_cols], kt_ref[0, c, qk_cols, :], v_ref[0, rows, v_cols],
                    gc_ref[0, rows, :], gr_ref[0, c], head,
                    state_ref.at[slot], m_ref.at[slot], reverse)
        return carry

    lax.fori_loop(0, n_chunks, step, 0)


def _mlstm_scan(q, kt, v, gcol, grow, batch, seq):
    tm = TM_ML
    cpt = tm // ML_CHUNK
    nt = seq // tm
    q3 = q.reshape(batch, seq, ML_QK_COLS)
    v3 = v.reshape(batch, seq, D_MODEL)
    kt4 = kt.reshape(batch, seq // ML_CHUNK, ML_QK_COLS, ML_CHUNK)
    gc3 = gcol.reshape(batch, seq, ML_GATE_COLS)
    gr4 = grow.reshape(batch, seq // ML_CHUNK, ML_GATE_COLS, ML_CHUNK)

    def specs(pos):
        return [pl.BlockSpec((1, tm, ML_QK_COLS), lambda b, i: (b, pos(i), 0)),
                pl.BlockSpec((1, cpt, ML_QK_COLS, ML_CHUNK), lambda b, i: (b, pos(i), 0, 0)),
                pl.BlockSpec((1, tm, D_MODEL), lambda b, i: (b, pos(i), 0)),
                pl.BlockSpec((1, tm, ML_GATE_COLS), lambda b, i: (b, pos(i), 0)),
                pl.BlockSpec((1, cpt, ML_GATE_COLS, ML_CHUNK), lambda b, i: (b, pos(i), 0, 0))]

    fwd = lambda i: i
    bwd = lambda i: nt - 1 - i
    h_shape = jax.ShapeDtypeStruct((batch, seq, D_MODEL), F32)
    n_slots = 2 * ML_HEADS
    return pl.pallas_call(
        _mlstm_scan_kernel,
        grid=(batch, nt),
        in_specs=specs(fwd) + specs(bwd),
        out_specs=[pl.BlockSpec((1, tm, D_MODEL), lambda b, i: (b, fwd(i), 0)),
                   pl.BlockSpec((1, tm, D_MODEL), lambda b, i: (b, bwd(i), 0))],
        out_shape=[h_shape, h_shape],
        scratch_shapes=[pltpu.VMEM((n_slots, ML_QK_DIM, ML_V_DIM + LANES), F32),
                        pltpu.VMEM((n_slots, 8, LANES), F32)],
        compiler_params=_params("parallel", "arbitrary"),
        name="mlstm_scan",
    )(q3, kt4, v3, gc3, gr4, q3, kt4, v3, gc3, gr4)


def _ffn_tail(x, h, g1, g2, g3, wgu_ref, wd_ref, out_ref):
    x1 = x + _rms(h, g1)
    xn = _rms(x1, g2).astype(BF16)
    gu = _dot(xn, wgu_ref[...])
    gate, up = gu[:, :FFN_HIDDEN], gu[:, FFN_HIDDEN:]
    act = (gate * jax.nn.sigmoid(gate) * up).astype(BF16)
    out_ref[...] = x1 + _rms(_dot(act, wd_ref[...]), g3)


def _attn_post_kernel(x_ref, a_ref, wo_ref, g_ref, wgu_ref, wd_ref, out_ref):
    h = _dot(a_ref[...], wo_ref[...])
    _ffn_tail(x_ref[...], h, g_ref[0:1], g_ref[1:2], g_ref[2:3], wgu_ref, wd_ref, out_ref)


def _mlstm_post_kernel(x_ref, hf_ref, hb_ref, o_ref, hg_ref, wo_ref, g_ref, wgu_ref, wd_ref,
                       out_ref):
    hsum = hf_ref[...] + hb_ref[...]
    heads = []
    for hd in range(ML_HEADS):
        cols = slice(hd * ML_V_DIM, (hd + 1) * ML_V_DIM)
        heads.append(_rms(hsum[:, cols], hg_ref[:, cols]))
    a = (jnp.concatenate(heads, axis=1) * jax.nn.sigmoid(o_ref[...])).astype(BF16)
    h = _dot(a, wo_ref[...])
    _ffn_tail(x_ref[...], h, g_ref[0:1], g_ref[1:2], g_ref[2:3], wgu_ref, wd_ref, out_ref)


def _pool_post_kernel(x_ref, prev_ref, next_ref, g0_ref, wp_ref, bp_ref, sp_ref, g_ref,
                      wgu_ref, wd_ref, out_ref, ext_ref, *, tiles_per_seq):
    tm = x_ref.shape[0]
    hl = POOL_HALO
    pos = pl.program_id(0) % tiles_per_seq
    x = x_ref[...]
    g0 = g0_ref[...]
    hn = _rms(x, g0)
    ext_ref[0:hl, :] = jnp.where(pos > 0, _rms(prev_ref[...], g0), 0.0)
    ext_ref[hl:hl + tm, :] = hn
    ext_ref[hl + tm:, :] = jnp.where(pos < tiles_per_seq - 1, _rms(next_ref[...], g0), 0.0)
    t = pos * tm + lax.broadcasted_iota(jnp.int32, (tm, 1), 0)
    seq = tiles_per_seq * tm
    parts = []
    for g, w in enumerate(POOL_WINDOWS):
        cols = slice(g * POOL_GROUP_DIM, (g + 1) * POOL_GROUP_DIM)
        tot = ext_ref[hl - w // 2:hl - w // 2 + tm, cols]
        for off in range(-w // 2 + 1, w // 2):
            tot = tot + ext_ref[hl + off:hl + off + tm, cols]
        cnt = (jnp.minimum(t + w // 2, seq) - jnp.maximum(t - w // 2, 0)).astype(F32)
        d = (tot / cnt - hn[:, cols]).astype(BF16)
        parts.append(_dot(d, wp_ref[g]))
    h = (jnp.concatenate(parts, axis=1) + bp_ref[...]) * sp_ref[...]
    _ffn_tail(x, h, g_ref[0:1], g_ref[1:2], g_ref[2:3], wgu_ref, wd_ref, out_ref)


def _ffn_specs():
    return [_const_spec((3, D_MODEL)), _const_spec((D_MODEL, 2 * FFN_HIDDEN)),
            _const_spec((FFN_HIDDEN, D_MODEL))]


def _attn_post(x2d, a, wo, gains, wgu, wd):
    t = x2d.shape[0]
    row = pl.BlockSpec((TM_FFN, D_MODEL), lambda i: (i, 0))
    return pl.pallas_call(
        _attn_post_kernel,
        grid=(t // TM_FFN,),
        in_specs=[row, row, _const_spec((D_MODEL, D_MODEL))] + _ffn_specs(),
        out_specs=row,
        out_shape=jax.ShapeDtypeStruct((t, D_MODEL), F32),
        compiler_params=_params("parallel"),
        name="attn_post_ffn",
    )(x2d, a, wo, gains, wgu, wd)


def _mlstm_post(x2d, hf, hb, o, head_gain, wo, gains, wgu, wd):
    t = x2d.shape[0]
    row = pl.BlockSpec((TM_FFN, D_MODEL), lambda i: (i, 0))
    return pl.pallas_call(
        _mlstm_post_kernel,
        grid=(t // TM_FFN,),
        in_specs=[row, row, row, row, _const_spec((1, D_MODEL)),
                  _const_spec((D_MODEL, D_MODEL))] + _ffn_specs(),
        out_specs=row,
        out_shape=jax.ShapeDtypeStruct((t, D_MODEL), F32),
        compiler_params=_params("parallel"),
        name="mlstm_post_ffn",
    )(x2d, hf, hb, o, head_gain, wo, gains, wgu, wd)


def _pool_post(x2d, g0, wp, bp, sp, gains, wgu, wd, seq):
    t = x2d.shape[0]
    tm = TM_FFN
    tps = seq // tm
    hpt = tm // POOL_HALO
    n_halo = t // POOL_HALO
    row = pl.BlockSpec((tm, D_MODEL), lambda i: (i, 0))
    prev = pl.BlockSpec((POOL_HALO, D_MODEL), lambda i: (jnp.maximum(i * hpt - 1, 0), 0))
    nxt = pl.BlockSpec((POOL_HALO, D_MODEL),
                       lambda i: (jnp.minimum((i + 1) * hpt, n_halo - 1), 0))
    kern = functools.partial(_pool_post_kernel, tiles_per_seq=tps)
    return pl.pallas_call(
        kern,
        grid=(t // tm,),
        in_specs=[row, prev, nxt, _const_spec((1, D_MODEL)),
                  _const_spec((POOL_GROUPS, POOL_GROUP_DIM, POOL_GROUP_DIM)),
                  _const_spec((1, D_MODEL)), _const_spec((1, D_MODEL))] + _ffn_specs(),
        out_specs=row,
        out_shape=jax.ShapeDtypeStruct((t, D_MODEL), F32),
        scratch_shapes=[pltpu.VMEM((tm + 2 * POOL_HALO, D_MODEL), F32)],
        compiler_params=_params("parallel"),
        name="pool_post_ffn",
    )(x2d, x2d, x2d, g0, wp, bp, sp, gains, wgu, wd)


def _rope_tables(seq):
    inv = ROPE_THETA ** (-jnp.arange(0, ROPE_DIM, 2, dtype=F32) / ROPE_DIM)
    ang = jnp.arange(seq, dtype=F32)[:, None] * inv[None, :]
    cos, sin = jnp.cos(ang), jnp.sin(ang)
    pad = DA_QK_DIM - ROPE_DIM
    cos_c = jnp.concatenate([cos, cos, jnp.ones((seq, pad), F32)], axis=1)
    sin_c = jnp.concatenate([-sin, sin, jnp.zeros((seq, pad), F32)], axis=1)
    return jnp.tile(cos_c, (1, 2)), jnp.tile(sin_c, (1, 2))


def _lambda_init(layer_idx):
    return 0.8 - 0.6 * math.exp(-0.3 * layer_idx)


def _trunk(x, w):
    batch, seq, _ = x.shape
    x2d = x.reshape(batch * seq, D_MODEL)
    cos_t, sin_t = _rope_tables(seq)
    for i in range(DEPTH):
        kind, slot = i % N_MIXERS, i // N_MIXERS
        g0 = w["norm_gain"][i, 0:1]
        gains = w["norm_gain"][i, 1:4]
        wgu, wd = w["w_ffn_gate_up"][i], w["w_ffn_down"][i]
        if kind == 0:
            q, k, vt = _attn_proj(x2d, g0, w["wq"][slot], w["wk"][slot], w["wvt"][slot],
                                  cos_t, sin_t, seq)
            a = _diff_attn(q, k, vt, w["attn_lambda"][slot], w["subln_col"][slot],
                           batch, seq, _lambda_init(i))
            x2d = _attn_post(x2d, a, w["w_attn_out"][slot], gains, wgu, wd)
        elif kind == 1:
            x2d = _pool_post(x2d, g0, w["w_pool"][slot], w["b_pool"][slot],
                             w["pool_scale"][slot], gains, wgu, wd, seq)
        else:
            q, kt, v, o, gc, gr = _mlstm_proj(
                x2d, g0, w["ml_wq"][slot], w["ml_wkt"][slot], w["ml_wv"][slot],
                w["ml_wo"][slot], w["ml_wg"][slot], w["ml_wgt"][slot],
                w["ml_bg"][slot], w["ml_bgt"][slot])
            hf, hb = _mlstm_scan(q, kt, v, gc, gr, batch, seq)
            x2d = _mlstm_post(x2d, hf.reshape(-1, D_MODEL), hb.reshape(-1, D_MODEL), o,
                              w["ml_head_gain"][slot], w["w_mlstm_out"][slot], gains, wgu, wd)
    return x2d.reshape(batch, seq, D_MODEL)


def _prepare_weights(norm_gain, w_attn_in, attn_lambda, attn_subln_gain, w_attn_out, w_pool,
                     b_pool, pool_scale, w_mlstm_in, b_mlstm_gate, mlstm_head_gain,
                     w_mlstm_out, w_ffn_gate_up, w_ffn_down):
    d, qk = D_MODEL, ML_QK_COLS
    swap = lambda a: jnp.swapaxes(a, -1, -2)
    wm = w_mlstm_in
    wg = wm[:, :, 2 * qk + 2 * d:]
    return {
        "norm_gain": norm_gain,
        "wq": w_attn_in[:, :, :d].astype(BF16),
        "wk": w_attn_in[:, :, d:2 * d].astype(BF16),
        "wvt": swap(w_attn_in[:, :, 2 * d:]).astype(BF16),
        "attn_lambda": attn_lambda,
        "subln_col": attn_subln_gain[:, :, None],
        "w_attn_out": w_attn_out.astype(BF16),
        "w_pool": w_pool.astype(BF16),
        "b_pool": b_pool[:, None, :],
        "pool_scale": pool_scale[:, None, :],
        "ml_wq": wm[:, :, :qk].astype(BF16),
        "ml_wkt": swap(wm[:, :, qk:2 * qk]).astype(BF16),
        "ml_wv": wm[:, :, 2 * qk:2 * qk + d].astype(BF16),
        "ml_wo": wm[:, :, 2 * qk + d:2 * qk + 2 * d].astype(BF16),
        "ml_wg": wg.astype(BF16),
        "ml_wgt": swap(wg).astype(BF16),
        "ml_bg": b_mlstm_gate[:, None, :],
        "ml_bgt": b_mlstm_gate[:, :, None],
        "ml_head_gain": mlstm_head_gain[:, None, :],
        "w_mlstm_out": w_mlstm_out.astype(BF16),
        "w_ffn_gate_up": w_ffn_gate_up.astype(BF16),
        "w_ffn_down": w_ffn_down.astype(BF16),
    }


def kernel(x_prompt, x_sample, norm_gain, w_attn_in, attn_lambda, attn_subln_gain, w_attn_out,
           w_pool, b_pool, pool_scale, w_mlstm_in, b_mlstm_gate, mlstm_head_gain, w_mlstm_out,
           w_ffn_gate_up, w_ffn_down):
    w = _prepare_weights(norm_gain, w_attn_in, attn_lambda, attn_subln_gain, w_attn_out,
                         w_pool, b_pool, pool_scale, w_mlstm_in, b_mlstm_gate,
                         mlstm_head_gain, w_mlstm_out, w_ffn_gate_up, w_ffn_down)
    return _trunk(x_prompt, w), _trunk(x_sample, w)
```

```python
import functools
import math

import jax
import jax.numpy as jnp
from jax import lax
from jax.experimental import pallas as pl
from jax.experimental.pallas import tpu as pltpu

D_MODEL = 1024
DEPTH = 4
N_MIXERS = 3
DA_HEADS = 8
DA_QK_DIM = 64
DA_V_DIM = 128
ROPE_DIM = 16
ROPE_THETA = 500000.0
POOL_GROUPS = 4
POOL_WINDOWS = (2, 4, 8, 16)
POOL_GROUP_DIM = D_MODEL // POOL_GROUPS
POOL_HALO = 8
ML_HEADS = 4
ML_QK_DIM = 128
ML_V_DIM = 256
ML_CHUNK = 128
ML_QK_COLS = ML_HEADS * ML_QK_DIM
ML_GATE_COLS = 4 * ML_HEADS
FFN_HIDDEN = 2816
NORM_EPS = 1e-6
LOG2E = math.log2(math.e)

LANES = 128
VMEM_LIMIT = 56 * 1024 * 1024
TM_PROJ = 512
TM_FFN = 256
TQ_ATTN = 256
TM_ML = 512

BF16 = jnp.bfloat16
F32 = jnp.float32


def _params(*sem):
    return pltpu.CompilerParams(dimension_semantics=sem, vmem_limit_bytes=VMEM_LIMIT)


def _const_spec(shape):
    nd = len(shape)
    return pl.BlockSpec(shape, lambda *_: (0,) * nd, pipeline_mode=pl.Buffered(1))


def _rms(x, gain):
    ms = jnp.mean(x * x, axis=-1, keepdims=True)
    return x * lax.rsqrt(ms + NORM_EPS) * gain


def _dot(a, b):
    return jnp.dot(a, b, preferred_element_type=F32)


def _dot_nt(a, b):
    return lax.dot_general(a, b, (((1,), (1,)), ((), ())), preferred_element_type=F32)


def _attn_proj_kernel(x_ref, g_ref, wq_ref, wk_ref, wvt_ref, cos_ref, sin_ref,
                      q_ref, k_ref, vt_ref):
    xn = _rms(x_ref[...], g_ref[...]).astype(BF16)
    cos = cos_ref[...]
    sin = sin_ref[...]
    lane = lax.broadcasted_iota(jnp.int32, (1, LANES), 1) % DA_QK_DIM
    first_half = lane < ROPE_DIM // 2

    def rope(y, scale):
        outs = []
        for h in range(DA_HEADS):
            slab = y[:, h * LANES:(h + 1) * LANES]
            partner = jnp.where(first_half,
                                pltpu.roll(slab, LANES - ROPE_DIM // 2, 1),
                                pltpu.roll(slab, ROPE_DIM // 2, 1))
            outs.append(((slab * cos + partner * sin) * scale).astype(BF16))
        return jnp.concatenate(outs, axis=1)

    q_ref[...] = rope(_dot(xn, wq_ref[...]), DA_QK_DIM ** -0.5 * LOG2E)
    k_ref[...] = rope(_dot(xn, wk_ref[...]), 1.0)
    vt_ref[0] = _dot_nt(wvt_ref[...], xn).astype(BF16)


def _attn_proj(x2d, gain, wq, wk, wvt, cos_t, sin_t, seq):
    t = x2d.shape[0]
    tm = TM_PROJ
    nps = seq // tm
    row = pl.BlockSpec((tm, D_MODEL), lambda i: (i, 0))
    tab = pl.BlockSpec((tm, LANES), lambda i: (i % nps, 0))
    return pl.pallas_call(
        _attn_proj_kernel,
        grid=(t // tm,),
        in_specs=[row, _const_spec((1, D_MODEL)), _const_spec((D_MODEL, D_MODEL)),
                  _const_spec((D_MODEL, D_MODEL)), _const_spec((D_MODEL, D_MODEL)), tab, tab],
        out_specs=[row, row, pl.BlockSpec((1, D_MODEL, tm), lambda i: (i, 0, 0))],
        out_shape=[jax.ShapeDtypeStruct((t, D_MODEL), BF16),
                   jax.ShapeDtypeStruct((t, D_MODEL), BF16),
                   jax.ShapeDtypeStruct((t // tm, D_MODEL, tm), BF16)],
        compiler_params=_params("parallel"),
        name="attn_proj",
    )(x2d, gain, wq, wk, wvt, cos_t, sin_t)


def _diff_attn_kernel(q_ref, k_ref, vt_ref, lam_ref, gain_ref, o_ref, acc_ref, sa_ref, sb_ref,
                      *, n_kv, tk, lambda_init):
    tq = q_ref.shape[1]
    q = q_ref[0]
    lane = lax.broadcasted_iota(jnp.int32, (1, LANES), 1)
    zero = jnp.zeros_like(q)
    qm = (jnp.where(lane < DA_QK_DIM, q, zero), jnp.where(lane >= DA_QK_DIM, q, zero))
    acc_ref[...] = jnp.zeros_like(acc_ref)

    def scores(j, dst_ref):
        kc = k_ref[0, pl.ds(pl.multiple_of(j * tk, tk), tk), :]
        for c in range(2):
            dst_ref[c] = _dot_nt(kc, qm[c])

    def consume(j, src_ref, carry):
        vt = vt_ref[0, j]
        new = []
        for c in range(2):
            m_old, l_old = carry[2 * c], carry[2 * c + 1]
            s = src_ref[c]
            m_new = jnp.maximum(m_old, jnp.max(s, axis=0, keepdims=True))
            alpha = jnp.exp2(m_old - m_new)
            p = jnp.exp2(s - m_new)
            l_new = alpha * l_old + jnp.sum(p, axis=0, keepdims=True)
            acc_ref[c] = alpha * acc_ref[c] + _dot(vt, p.astype(BF16))
            new += [m_new, l_new]
        return tuple(new)

    def pair(i, carry, last=False):
        j = 2 * i
        scores(j + 1, sb_ref)
        carry = consume(j, sa_ref, carry)
        if not last:
            scores(j + 2, sa_ref)
        return consume(j + 1, sb_ref, carry)

    neg = jnp.full((1, tq), -jnp.inf, F32)
    zer = jnp.zeros((1, tq), F32)
    n_pairs = n_kv // 2
    scores(0, sa_ref)
    carry = lax.fori_loop(0, n_pairs - 1, pair, (neg, zer, neg, zer))
    _, l1, _, l2 = pair(n_pairs - 1, carry, last=True)

    lp = lam_ref[...]
    lam = (jnp.exp(jnp.sum(lp[0:1] * lp[1:2], keepdims=True))
           - jnp.exp(jnp.sum(lp[2:3] * lp[3:4], keepdims=True)) + lambda_init)
    o = acc_ref[0] / l1 - lam * (acc_ref[1] / l2)
    ms = jnp.mean(o * o, axis=0, keepdims=True)
    y = o * lax.rsqrt(ms + NORM_EPS) * gain_ref[...] * (1.0 - lambda_init)
    o_ref[0] = y.T.astype(BF16)


def _diff_attn(q, k, vt, lam_params, gain_col, batch, seq, lambda_init):
    tq, tk = TQ_ATTN, TM_PROJ
    n_kv = seq // tk
    q3 = q.reshape(batch, seq, D_MODEL)
    k3 = k.reshape(batch, seq, D_MODEL)
    vt4 = vt.reshape(batch, n_kv, D_MODEL, tk)
    kern = functools.partial(_diff_attn_kernel, n_kv=n_kv, tk=tk, lambda_init=lambda_init)
    out = pl.pallas_call(
        kern,
        grid=(batch, DA_HEADS, seq // tq),
        in_specs=[pl.BlockSpec((1, tq, LANES), lambda b, h, i: (b, i, h)),
                  pl.BlockSpec((1, seq, LANES), lambda b, h, i: (b, 0, h)),
                  pl.BlockSpec((1, n_kv, DA_V_DIM, tk), lambda b, h, i: (b, 0, h, 0)),
                  pl.BlockSpec((4, DA_QK_DIM), lambda b, h, i: (0, 0)),
                  pl.BlockSpec((DA_V_DIM, 1), lambda b, h, i: (0, 0))],
        out_specs=pl.BlockSpec((1, tq, LANES), lambda b, h, i: (b, i, h)),
        out_shape=jax.ShapeDtypeStruct((batch, seq, D_MODEL), BF16),
        scratch_shapes=[pltpu.VMEM((2, DA_V_DIM, tq), F32),
                        pltpu.VMEM((2, tk, tq), F32), pltpu.VMEM((2, tk, tq), F32)],
        compiler_params=_params("parallel", "parallel", "arbitrary"),
        name="diff_attn",
    )(q3, k3, vt4, lam_params, gain_col)
    return out.reshape(batch * seq, D_MODEL)


def _mlstm_proj_kernel(x_ref, g_ref, wq_ref, wkt_ref, wv_ref, wo_ref, wg_ref, wgt_ref,
                       bg_ref, bgt_ref, q_ref, kt_ref, v_ref, o_ref, gc_ref, gr_ref):
    xn = _rms(x_ref[...], g_ref[...]).astype(BF16)
    q_ref[...] = _dot(xn, wq_ref[...]).astype(BF16)
    kt = (_dot_nt(wkt_ref[...], xn) * ML_QK_DIM ** -0.5).astype(BF16)
    v_ref[...] = _dot(xn, wv_ref[...]).astype(BF16)
    o_ref[...] = _dot(xn, wo_ref[...])

    def gate_act(g, is_forget):
        log_sig = jnp.minimum(g, 0.0) - jnp.log1p(jnp.exp(-jnp.abs(g)))
        return jnp.where(is_forget, log_sig, g)

    gc = _dot(xn, wg_ref[...]) + bg_ref[...]
    col_kind = (lax.broadcasted_iota(jnp.int32, (1, ML_GATE_COLS), 1) // ML_HEADS) % 2
    gc_ref[...] = gate_act(gc, col_kind == 1)
    gr = _dot_nt(wgt_ref[...], xn) + bgt_ref[...]
    row_kind = (lax.broadcasted_iota(jnp.int32, (ML_GATE_COLS, 1), 0) // ML_HEADS) % 2
    gr = gate_act(gr, row_kind == 1)
    for c in range(x_ref.shape[0] // ML_CHUNK):
        kt_ref[0, c] = kt[:, c * ML_CHUNK:(c + 1) * ML_CHUNK]
        gr_ref[0, c] = gr[:, c * ML_CHUNK:(c + 1) * ML_CHUNK]


def _mlstm_proj(x2d, gain, wq, wkt, wv, wo, wg, wgt, bg, bgt):
    t = x2d.shape[0]
    tm = TM_PROJ
    cpt = tm // ML_CHUNK
    row = lambda n: pl.BlockSpec((tm, n), lambda i: (i, 0))
    return pl.pallas_call(
        _mlstm_proj_kernel,
        grid=(t // tm,),
        in_specs=[row(D_MODEL), _const_spec((1, D_MODEL)),
                  _const_spec((D_MODEL, ML_QK_COLS)), _const_spec((ML_QK_COLS, D_MODEL)),
                  _const_spec((D_MODEL, D_MODEL)), _const_spec((D_MODEL, D_MODEL)),
                  _const_spec((D_MODEL, ML_GATE_COLS)), _const_spec((ML_GATE_COLS, D_MODEL)),
                  _const_spec((1, ML_GATE_COLS)), _const_spec((ML_GATE_COLS, 1))],
        out_specs=[row(ML_QK_COLS),
                   pl.BlockSpec((1, cpt, ML_QK_COLS, ML_CHUNK), lambda i: (i, 0, 0, 0)),
                   row(D_MODEL), row(D_MODEL), row(ML_GATE_COLS),
                   pl.BlockSpec((1, cpt, ML_GATE_COLS, ML_CHUNK), lambda i: (i, 0, 0, 0))],
        out_shape=[jax.ShapeDtypeStruct((t, ML_QK_COLS), BF16),
                   jax.ShapeDtypeStruct((t // tm, cpt, ML_QK_COLS, ML_CHUNK), BF16),
                   jax.ShapeDtypeStruct((t, D_MODEL), BF16),
                   jax.ShapeDtypeStruct((t, D_MODEL), F32),
                   jax.ShapeDtypeStruct((t, ML_GATE_COLS), F32),
                   jax.ShapeDtypeStruct((t // tm, cpt, ML_GATE_COLS, ML_CHUNK), F32)],
        compiler_params=_params("parallel"),
        name="mlstm_proj",
    )(x2d, gain, wq, wkt, wv, wo, wg, wgt, bg, bgt)


def _mlstm_chunk(q, kt, v, gcol, grow, head, state_ref, m_ref, reverse):
    L = ML_CHUNK
    gi = (2 if reverse else 0) * ML_HEADS + head
    gf = gi + ML_HEADS
    i_col, f_col = gcol[:, gi:gi + 1], gcol[:, gf:gf + 1]
    i_row, f_row = grow[gi:gi + 1, :], grow[gf:gf + 1, :]
    r = lax.broadcasted_iota(jnp.int32, (L, L), 0)
    c = lax.broadcasted_iota(jnp.int32, (L, L), 1)
    keep = (c >= r) if reverse else (c <= r)
    zero = jnp.zeros((L, L), F32)
    b_col = jnp.sum(jnp.where(keep, f_row, zero), axis=1, keepdims=True)
    b_row = jnp.sum(jnp.where(keep, zero, f_col) + jnp.where(r == c, f_col, zero),
                    axis=0, keepdims=True)
    b_last = jnp.sum(f_row, axis=1, keepdims=True)
    m_old = m_ref[0:1, 0:1]

    dlog = jnp.where(keep, b_col - b_row + i_row, -jnp.inf)
    g = b_col + m_old
    m_row = jnp.maximum(g, jnp.max(dlog, axis=1, keepdims=True))
    w_intra = jnp.exp(dlog - m_row)
    w_inter = jnp.exp(g - m_row)

    s = _dot(q, kt) * w_intra
    state = state_ref[...]
    inter = _dot(q, state.astype(BF16))
    num = w_inter * inter[:, :ML_V_DIM] + _dot(s.astype(BF16), v)
    den = w_inter * inter[:, ML_V_DIM:ML_V_DIM + 1] + jnp.sum(s, axis=1, keepdims=True)
    h = num / jnp.maximum(jnp.abs(den), jnp.exp(-m_row))

    w_state_log = b_last - b_col + i_col
    m_new = jnp.maximum(b_last + m_old, jnp.max(w_state_log, axis=0, keepdims=True))
    w_state = jnp.exp(w_state_log - m_new)
    decay = jnp.exp(b_last + m_old - m_new)
    ones_col = (lax.broadcasted_iota(jnp.int32, (L, LANES), 1) == 0).astype(F32)
    v_aug = jnp.concatenate([v.astype(F32), ones_col], axis=1)
    state_ref[...] = decay * state + _dot(kt, (w_state * v_aug).astype(BF16))
    m_ref[...] = jnp.broadcast_to(m_new, m_ref.shape)
    return h


def _mlstm_scan_kernel(qf_ref, ktf_ref, vf_ref, gcf_ref, grf_ref,
                       qb_ref, ktb_ref, vb_ref, gcb_ref, grb_ref,
                       hf_ref, hb_ref, state_ref, m_ref):
    @pl.when(pl.program_id(1) == 0)
    def _():
        state_ref[...] = jnp.zeros_like(state_ref)
        m_ref[...] = jnp.zeros_like(m_ref)

    n_chunks = qf_ref.shape[1] // ML_CHUNK
    fwd_refs = (qf_ref, ktf_ref, vf_ref, gcf_ref, grf_ref, hf_ref)
    bwd_refs = (qb_ref, ktb_ref, vb_ref, gcb_ref, grb_ref, hb_ref)

    def step(i, carry):
        for head in range(ML_HEADS):
            for reverse in (False, True):
                q_ref, kt_ref, v_ref, gc_ref, gr_ref, h_ref = bwd_refs if reverse else fwd_refs
                c = n_chunks - 1 - i if reverse else i
                rows = pl.ds(pl.multiple_of(c * ML_CHUNK, ML_CHUNK), ML_CHUNK)
                qk_cols = slice(head * ML_QK_DIM, (head + 1) * ML_QK_DIM)
                v_cols = slice(head * ML_V_DIM, (head + 1) * ML_V_DIM)
                slot = head * 2 + int(reverse)
                h_ref[0, rows, v_cols] = _mlstm_chunk(
                    q_ref[0, rows, qk_cols], kt_ref[0, c, qk_cols, :], v_ref[0, rows, v_cols],
                    gc_ref[0, rows, :], gr_ref[0, c], head,
                    state_ref.at[slot], m_ref.at[slot], reverse)
        return carry

    lax.fori_loop(0, n_chunks, step, 0)


def _mlstm_scan(q, kt, v, gcol, grow, batch, seq):
    tm = TM_ML
    cpt = tm // ML_CHUNK
    nt = seq // tm
    q3 = q.reshape(batch, seq, ML_QK_COLS)
    v3 = v.reshape(batch, seq, D_MODEL)
    kt4 = kt.reshape(batch, seq // ML_CHUNK, ML_QK_COLS, ML_CHUNK)
    gc3 = gcol.reshape(batch, seq, ML_GATE_COLS)
    gr4 = grow.reshape(batch, seq // ML_CHUNK, ML_GATE_COLS, ML_CHUNK)

    def specs(pos):
        return [pl.BlockSpec((1, tm, ML_QK_COLS), lambda b, i: (b, pos(i), 0)),
                pl.BlockSpec((1, cpt, ML_QK_COLS, ML_CHUNK), lambda b, i: (b, pos(i), 0, 0)),
                pl.BlockSpec((1, tm, D_MODEL), lambda b, i: (b, pos(i), 0)),
                pl.BlockSpec((1, tm, ML_GATE_COLS), lambda b, i: (b, pos(i), 0)),
                pl.BlockSpec((1, cpt, ML_GATE_COLS, ML_CHUNK), lambda b, i: (b, pos(i), 0, 0))]

    fwd = lambda i: i
    bwd = lambda i: nt - 1 - i
    h_shape = jax.ShapeDtypeStruct((batch, seq, D_MODEL), F32)
    n_slots = 2 * ML_HEADS
    return pl.pallas_call(
        _mlstm_scan_kernel,
        grid=(batch, nt),
        in_specs=specs(fwd) + specs(bwd),
        out_specs=[pl.BlockSpec((1, tm, D_MODEL), lambda b, i: (b, fwd(i), 0)),
                   pl.BlockSpec((1, tm, D_MODEL), lambda b, i: (b, bwd(i), 0))],
        out_shape=[h_shape, h_shape],
        scratch_shapes=[pltpu.VMEM((n_slots, ML_QK_DIM, ML_V_DIM + LANES), F32),
                        pltpu.VMEM((n_slots, 8, LANES), F32)],
        compiler_params=_params("parallel", "arbitrary"),
        name="mlstm_scan",
    )(q3, kt4, v3, gc3, gr4, q3, kt4, v3, gc3, gr4)


def _ffn_tail(x, h, g1, g2, g3, wgu_ref, wd_ref, out_ref):
    x1 = x + _rms(h, g1)
    xn = _rms(x1, g2).astype(BF16)
    gu = _dot(xn, wgu_ref[...])
    gate, up = gu[:, :FFN_HIDDEN], gu[:, FFN_HIDDEN:]
    act = (gate * jax.nn.sigmoid(gate) * up).astype(BF16)
    out_ref[...] = x1 + _rms(_dot(act, wd_ref[...]), g3)


def _attn_post_kernel(x_ref, a_ref, wo_ref, g_ref, wgu_ref, wd_ref, out_ref):
    h = _dot(a_ref[...], wo_ref[...])
    _ffn_tail(x_ref[...], h, g_ref[0:1], g_ref[1:2], g_ref[2:3], wgu_ref, wd_ref, out_ref)


def _mlstm_post_kernel(x_ref, hf_ref, hb_ref, o_ref, hg_ref, wo_ref, g_ref, wgu_ref, wd_ref,
                       out_ref):
    hsum = hf_ref[...] + hb_ref[...]
    heads = []
    for hd in range(ML_HEADS):
        cols = slice(hd * ML_V_DIM, (hd + 1) * ML_V_DIM)
        heads.append(_rms(hsum[:, cols], hg_ref[:, cols]))
    a = (jnp.concatenate(heads, axis=1) * jax.nn.sigmoid(o_ref[...])).astype(BF16)
    h = _dot(a, wo_ref[...])
    _ffn_tail(x_ref[...], h, g_ref[0:1], g_ref[1:2], g_ref[2:3], wgu_ref, wd_ref, out_ref)


def _pool_post_kernel(x_ref, prev_ref, next_ref, g0_ref, wp_ref, bp_ref, sp_ref, g_ref,
                      wgu_ref, wd_ref, out_ref, ext_ref, *, tiles_per_seq):
    tm = x_ref.shape[0]
    hl = POOL_HALO
    pos = pl.program_id(0) % tiles_per_seq
    x = x_ref[...]
    g0 = g0_ref[...]
    hn = _rms(x, g0)
    ext_ref[0:hl, :] = jnp.where(pos > 0, _rms(prev_ref[...], g0), 0.0)
    ext_ref[hl:hl + tm, :] = hn
    ext_ref[hl + tm:, :] = jnp.where(pos < tiles_per_seq - 1, _rms(next_ref[...], g0), 0.0)
    t = pos * tm + lax.broadcasted_iota(jnp.int32, (tm, 1), 0)
    seq = tiles_per_seq * tm
    parts = []
    for g, w in enumerate(POOL_WINDOWS):
        cols = slice(g * POOL_GROUP_DIM, (g + 1) * POOL_GROUP_DIM)
        tot = ext_ref[hl - w // 2:hl - w // 2 + tm, cols]
        for off in range(-w // 2 + 1, w // 2):
            tot = tot + ext_ref[hl + off:hl + off + tm, cols]
        cnt = (jnp.minimum(t + w // 2, seq) - jnp.maximum(t - w // 2, 0)).astype(F32)
        d = (tot / cnt - hn[:, cols]).astype(BF16)
        parts.append(_dot(d, wp_ref[g]))
    h = (jnp.concatenate(parts, axis=1) + bp_ref[...]) * sp_ref[...]
    _ffn_tail(x, h, g_ref[0:1], g_ref[1:2], g_ref[2:3], wgu_ref, wd_ref, out_ref)


def _ffn_specs():
    return [_const_spec((3, D_MODEL)), _const_spec((D_MODEL, 2 * FFN_HIDDEN)),
            _const_spec((FFN_HIDDEN, D_MODEL))]


def _attn_post(x2d, a, wo, gains, wgu, wd):
    t = x2d.shape[0]
    row = pl.BlockSpec((TM_FFN, D_MODEL), lambda i: (i, 0))
    return pl.pallas_call(
        _attn_post_kernel,
        grid=(t // TM_FFN,),
        in_specs=[row, row, _const_spec((D_MODEL, D_MODEL))] + _ffn_specs(),
        out_specs=row,
        out_shape=jax.ShapeDtypeStruct((t, D_MODEL), F32),
        compiler_params=_params("parallel"),
        name="attn_post_ffn",
    )(x2d, a, wo, gains, wgu, wd)


def _mlstm_post(x2d, hf, hb, o, head_gain, wo, gains, wgu, wd):
    t = x2d.shape[0]
    row = pl.BlockSpec((TM_FFN, D_MODEL), lambda i: (i, 0))
    return pl.pallas_call(
        _mlstm_post_kernel,
        grid=(t // TM_FFN,),
        in_specs=[row, row, row, row, _const_spec((1, D_MODEL)),
                  _const_spec((D_MODEL, D_MODEL))] + _ffn_specs(),
        out_specs=row,
        out_shape=jax.ShapeDtypeStruct((t, D_MODEL), F32),
        compiler_params=_params("parallel"),
        name="mlstm_post_ffn",
    )(x2d, hf, hb, o, head_gain, wo, gains, wgu, wd)


def _pool_post(x2d, g0, wp, bp, sp, gains, wgu, wd, seq):
    t = x2d.shape[0]
    tm = TM_FFN
    tps = seq // tm
    hpt = tm // POOL_HALO
    n_halo = t // POOL_HALO
    row = pl.BlockSpec((tm, D_MODEL), lambda i: (i, 0))
    prev = pl.BlockSpec((POOL_HALO, D_MODEL), lambda i: (jnp.maximum(i * hpt - 1, 0), 0))
    nxt = pl.BlockSpec((POOL_HALO, D_MODEL),
                       lambda i: (jnp.minimum((i + 1) * hpt, n_halo - 1), 0))
    kern = functools.partial(_pool_post_kernel, tiles_per_seq=tps)
    return pl.pallas_call(
        kern,
        grid=(t // tm,),
        in_specs=[row, prev, nxt, _const_spec((1, D_MODEL)),
                  _const_spec((POOL_GROUPS, POOL_GROUP_DIM, POOL_GROUP_DIM)),
                  _const_spec((1, D_MODEL)), _const_spec((1, D_MODEL))] + _ffn_specs(),
        out_specs=row,
        out_shape=jax.ShapeDtypeStruct((t, D_MODEL), F32),
        scratch_shapes=[pltpu.VMEM((tm + 2 * POOL_HALO, D_MODEL), F32)],
        compiler_params=_params("parallel"),
        name="pool_post_ffn",
    )(x2d, x2d, x2d, g0, wp, bp, sp, gains, wgu, wd)


def _rope_tables(seq):
    inv = ROPE_THETA ** (-jnp.arange(0, ROPE_DIM, 2, dtype=F32) / ROPE_DIM)
    ang = jnp.arange(seq, dtype=F32)[:, None] * inv[None, :]
    cos, sin = jnp.cos(ang), jnp.sin(ang)
    pad = DA_QK_DIM - ROPE_DIM
    cos_c = jnp.concatenate([cos, cos, jnp.ones((seq, pad), F32)], axis=1)
    sin_c = jnp.concatenate([-sin, sin, jnp.zeros((seq, pad), F32)], axis=1)
    return jnp.tile(cos_c, (1, 2)), jnp.tile(sin_c, (1, 2))


def _lambda_init(layer_idx):
    return 0.8 - 0.6 * math.exp(-0.3 * layer_idx)


def _trunk(x, w):
    batch, seq, _ = x.shape
    x2d = x.reshape(batch * seq, D_MODEL)
    cos_t, sin_t = _rope_tables(seq)
    for i in range(DEPTH):
        kind, slot = i % N_MIXERS, i // N_MIXERS
        g0 = w["norm_gain"][i, 0:1]
        gains = w["norm_gain"][i, 1:4]
        wgu, wd = w["w_ffn_gate_up"][i], w["w_ffn_down"][i]
        if kind == 0:
            q, k, vt = _attn_proj(x2d, g0, w["wq"][slot], w["wk"][slot], w["wvt"][slot],
                                  cos_t, sin_t, seq)
            a = _diff_attn(q, k, vt, w["attn_lambda"][slot], w["subln_col"][slot],
                           batch, seq, _lambda_init(i))
            x2d = _attn_post(x2d, a, w["w_attn_out"][slot], gains, wgu, wd)
        elif kind == 1:
            x2d = _pool_post(x2d, g0, w["w_pool"][slot], w["b_pool"][slot],
                             w["pool_scale"][slot], gains, wgu, wd, seq)
        else:
            q, kt, v, o, gc, gr = _mlstm_proj(
                x2d, g0, w["ml_wq"][slot], w["ml_wkt"][slot], w["ml_wv"][slot],
                w["ml_wo"][slot], w["ml_wg"][slot], w["ml_wgt"][slot],
                w["ml_bg"][slot], w["ml_bgt"][slot])
            hf, hb = _mlstm_scan(q, kt, v, gc, gr, batch, seq)
            x2d = _mlstm_post(x2d, hf.reshape(-1, D_MODEL), hb.reshape(-1, D_MODEL), o,
                              w["ml_head_gain"][slot], w["w_mlstm_out"][slot], gains, wgu, wd)
    return x2d.reshape(batch, seq, D_MODEL)


def _prepare_weights(norm_gain, w_attn_in, attn_lambda, attn_subln_gain, w_attn_out, w_pool,
                     b_pool, pool_scale, w_mlstm_in, b_mlstm_gate, mlstm_head_gain,
                     w_mlstm_out, w_ffn_gate_up, w_ffn_down):
    d, qk = D_MODEL, ML_QK_COLS
    swap = lambda a: jnp.swapaxes(a, -1, -2)
    wm = w_mlstm_in
    wg = wm[:, :, 2 * qk + 2 * d:]
    return {
        "norm_gain": norm_gain,
        "wq": w_attn_in[:, :, :d].astype(BF16),
        "wk": w_attn_in[:, :, d:2 * d].astype(BF16),
        "wvt": swap(w_attn_in[:, :, 2 * d:]).astype(BF16),
        "attn_lambda": attn_lambda,
        "subln_col": attn_subln_gain[:, :, None],
        "w_attn_out": w_attn_out.astype(BF16),
        "w_pool": w_pool.astype(BF16),
        "b_pool": b_pool[:, None, :],
        "pool_scale": pool_scale[:, None, :],
        "ml_wq": wm[:, :, :qk].astype(BF16),
        "ml_wkt": swap(wm[:, :, qk:2 * qk]).astype(BF16),
        "ml_wv": wm[:, :, 2 * qk:2 * qk + d].astype(BF16),
        "ml_wo": wm[:, :, 2 * qk + d:2 * qk + 2 * d].astype(BF16),
        "ml_wg": wg.astype(BF16),
        "ml_wgt": swap(wg).astype(BF16),
        "ml_bg": b_mlstm_gate[:, None, :],
        "ml_bgt": b_mlstm_gate[:, :, None],
        "ml_head_gain": mlstm_head_gain[:, None, :],
        "w_mlstm_out": w_mlstm_out.astype(BF16),
        "w_ffn_gate_up": w_ffn_gate_up.astype(BF16),
        "w_ffn_down": w_ffn_down.astype(BF16),
    }


def kernel(x_prompt, x_sample, norm_gain, w_attn_in, attn_lambda, attn_subln_gain, w_attn_out,
           w_pool, b_pool, pool_scale, w_mlstm_in, b_mlstm_gate, mlstm_head_gain, w_mlstm_out,
           w_ffn_gate_up, w_ffn_down):
    w = _prepare_weights(norm_gain, w_attn_in, attn_lambda, attn_subln_gain, w_attn_out,
                         w_pool, b_pool, pool_scale, w_mlstm_in, b_mlstm_gate,
                         mlstm_head_gain, w_mlstm_out, w_ffn_gate_up, w_ffn_down)
    return _trunk(x_prompt, w), _trunk(x_sample, w)
```

```python
import functools
import math

import jax
import jax.numpy as jnp
from jax import lax
from jax.experimental import pallas as pl
from jax.experimental.pallas import tpu as pltpu

D_MODEL = 1024
DEPTH = 4
N_MIXERS = 3
DA_HEADS = 8
DA_QK_DIM = 64
DA_V_DIM = 128
ROPE_DIM = 16
ROPE_THETA = 500000.0
POOL_GROUPS = 4
POOL_WINDOWS = (2, 4, 8, 16)
POOL_GROUP_DIM = D_MODEL // POOL_GROUPS
POOL_HALO = 8
ML_HEADS = 4
ML_QK_DIM = 128
ML_V_DIM = 256
ML_CHUNK = 128
ML_QK_COLS = ML_HEADS * ML_QK_DIM
ML_GATE_COLS = 4 * ML_HEADS
FFN_HIDDEN = 2816
NORM_EPS = 1e-6
LOG2E = math.log2(math.e)

LANES = 128
VMEM_LIMIT = 56 * 1024 * 1024
TM_PROJ = 512
TM_FFN = 512
FFN_SPLITS = ((0, 1536), (1536, FFN_HIDDEN))
TQ_ATTN = 256
TM_ML = 512

BF16 = jnp.bfloat16
F32 = jnp.float32


def _params(*sem):
    return pltpu.CompilerParams(dimension_semantics=sem, vmem_limit_bytes=VMEM_LIMIT)


def _const_spec(shape):
    nd = len(shape)
    return pl.BlockSpec(shape, lambda *_: (0,) * nd, pipeline_mode=pl.Buffered(1))


def _rms(x, gain):
    ms = jnp.mean(x * x, axis=-1, keepdims=True)
    return x * lax.rsqrt(ms + NORM_EPS) * gain


def _dot(a, b):
    return jnp.dot(a, b, preferred_element_type=F32)


def _dot_nt(a, b):
    return lax.dot_general(a, b, (((1,), (1,)), ((), ())), preferred_element_type=F32)


def _attn_proj_kernel(x_ref, g_ref, wq_ref, wk_ref, wvt_ref, cos_ref, sin_ref,
                      q_ref, k_ref, vt_ref):
    xn = _rms(x_ref[...], g_ref[...]).astype(BF16)
    cos = cos_ref[...]
    sin = sin_ref[...]
    lane = lax.broadcasted_iota(jnp.int32, (1, LANES), 1) % DA_QK_DIM
    first_half = lane < ROPE_DIM // 2

    def rope(y, scale):
        outs = []
        for h in range(DA_HEADS):
            slab = y[:, h * LANES:(h + 1) * LANES]
            partner = jnp.where(first_half,
                                pltpu.roll(slab, LANES - ROPE_DIM // 2, 1),
                                pltpu.roll(slab, ROPE_DIM // 2, 1))
            outs.append(((slab * cos + partner * sin) * scale).astype(BF16))
        return jnp.concatenate(outs, axis=1)

    q_ref[...] = rope(_dot(xn, wq_ref[...]), DA_QK_DIM ** -0.5 * LOG2E)
    k_ref[...] = rope(_dot(xn, wk_ref[...]), 1.0)
    vt_ref[0] = _dot_nt(wvt_ref[...], xn).astype(BF16)


def _attn_proj(x2d, gain, wq, wk, wvt, cos_t, sin_t, seq):
    t = x2d.shape[0]
    tm = TM_PROJ
    nps = seq // tm
    row = pl.BlockSpec((tm, D_MODEL), lambda i: (i, 0))
    tab = pl.BlockSpec((tm, LANES), lambda i: (i % nps, 0))
    return pl.pallas_call(
        _attn_proj_kernel,
        grid=(t // tm,),
        in_specs=[row, _const_spec((1, D_MODEL)), _const_spec((D_MODEL, D_MODEL)),
                  _const_spec((D_MODEL, D_MODEL)), _const_spec((D_MODEL, D_MODEL)), tab, tab],
        out_specs=[row, row, pl.BlockSpec((1, D_MODEL, tm), lambda i: (i, 0, 0))],
        out_shape=[jax.ShapeDtypeStruct((t, D_MODEL), BF16),
                   jax.ShapeDtypeStruct((t, D_MODEL), BF16),
                   jax.ShapeDtypeStruct((t // tm, D_MODEL, tm), BF16)],
        compiler_params=_params("parallel"),
        name="attn_proj",
    )(x2d, gain, wq, wk, wvt, cos_t, sin_t)


DV_AUG = DA_V_DIM + 16


def _diff_attn_kernel(q_ref, k_ref, vt_ref, lam_ref, gain_ref, o_ref, acc_ref, fin_ref,
                      sa_ref, sb_ref, *, n_kv, tk, tq, unroll, q_unroll, lambda_init):
    n_q = q_ref.shape[1] // tq
    n_pairs = n_kv // 2
    lane = lax.broadcasted_iota(jnp.int32, (1, LANES), 1)
    lp = lam_ref[...]
    lam = (jnp.exp(jnp.sum(lp[0:1] * lp[1:2], keepdims=True))
           - jnp.exp(jnp.sum(lp[2:3] * lp[3:4], keepdims=True)) + lambda_init)
    out_gain = gain_ref[...] * (1.0 - lambda_init)
    ones_rows = jnp.ones((DV_AUG - DA_V_DIM, tk), BF16)
    neg = jnp.full((1, tq), -jnp.inf, F32)

    def masked_q(qi):
        q = q_ref[0, pl.ds(pl.multiple_of(qi * tq, tq), tq), :]
        zero = jnp.zeros_like(q)
        return jnp.where(lane < DA_QK_DIM, q, zero), jnp.where(lane >= DA_QK_DIM, q, zero)

    def scores(qm, j, dst_ref):
        kc = k_ref[0, pl.ds(pl.multiple_of(j * tk, tk), tk), :]
        for c in range(2):
            dst_ref[c] = _dot_nt(kc, qm[c])

    def consume(j, src_ref, m):
        vt = jnp.concatenate([vt_ref[0, j], ones_rows], axis=0)
        m_out = []
        for c in range(2):
            s = src_ref[c]
            m_new = jnp.maximum(m[c], jnp.max(s, axis=0, keepdims=True))
            alpha = jnp.exp2(m[c] - m_new)
            p = jnp.exp2(s - m_new).astype(BF16)
            acc_ref[c] = alpha * acc_ref[c] + _dot(vt, p)
            m_out.append(m_new)
        return tuple(m_out)

    def finalize(qi):
        a1, a2 = fin_ref[0], fin_ref[1]
        o = (a1[:DA_V_DIM] / a1[DA_V_DIM:DA_V_DIM + 1]
             - lam * (a2[:DA_V_DIM] / a2[DA_V_DIM:DA_V_DIM + 1]))
        ms = jnp.mean(o * o, axis=0, keepdims=True)
        y = o * lax.rsqrt(ms + NORM_EPS) * out_gain
        start = qi * tq if isinstance(qi, int) else pl.multiple_of(qi * tq, tq)
        o_ref[0, pl.ds(start, tq), :] = y.T.astype(BF16)

    def q_tile(qi, carry):
        qm = masked_q(qi)
        qm_next = masked_q(jnp.minimum(qi + 1, n_q - 1))
        finalize(jnp.maximum(qi - 1, 0))

        def pair(i, m):
            j = 2 * i
            scores(qm, j + 1, sb_ref)
            m = consume(j, sa_ref, m)
            scores(qm, j + 2, sa_ref)
            return consume(j + 1, sb_ref, m)

        m = lax.fori_loop(0, n_pairs - 1, pair, (neg, neg), unroll=unroll)
        scores(qm, n_kv - 1, sb_ref)
        m = consume(n_kv - 2, sa_ref, m)
        scores(qm_next, 0, sa_ref)
        consume(n_kv - 1, sb_ref, m)
        fin_ref[...] = acc_ref[...]
        return carry

    acc_ref[...] = jnp.zeros_like(acc_ref)
    fin_ref[...] = jnp.ones_like(fin_ref)
    scores(masked_q(0), 0, sa_ref)
    lax.fori_loop(0, n_q, q_tile, 0, unroll=q_unroll)
    finalize(n_q - 1)


def _diff_attn(q, k, vt, lam_params, gain_col, batch, seq, lambda_init):
    tq, tk = TQ_ATTN, TM_PROJ
    n_kv = seq // tk
    assert n_kv % 2 == 0 and seq % tq == 0
    n_loop = n_kv // 2 - 1
    unroll = True if n_loop <= 1 else (5 if n_loop % 5 == 0 else 1)
    q3 = q.reshape(batch, seq, D_MODEL)
    k3 = k.reshape(batch, seq, D_MODEL)
    vt4 = vt.reshape(batch, n_kv, D_MODEL, tk)
    q_unroll = 4 if n_loop <= 1 and (seq // tq) % 4 == 0 else 1
    kern = functools.partial(_diff_attn_kernel, n_kv=n_kv, tk=tk, tq=tq, unroll=unroll,
                             q_unroll=q_unroll, lambda_init=lambda_init)
    head = pl.BlockSpec((1, seq, LANES), lambda b, h: (b, 0, h))
    out = pl.pallas_call(
        kern,
        grid=(batch, DA_HEADS),
        in_specs=[head, head,
                  pl.BlockSpec((1, n_kv, DA_V_DIM, tk), lambda b, h: (b, 0, h, 0)),
                  pl.BlockSpec((4, DA_QK_DIM), lambda b, h: (0, 0)),
                  pl.BlockSpec((DA_V_DIM, 1), lambda b, h: (0, 0))],
        out_specs=head,
        out_shape=jax.ShapeDtypeStruct((batch, seq, D_MODEL), BF16),
        scratch_shapes=[pltpu.VMEM((2, DV_AUG, tq), F32), pltpu.VMEM((2, DV_AUG, tq), F32),
                        pltpu.VMEM((2, tk, tq), F32), pltpu.VMEM((2, tk, tq), F32)],
        compiler_params=_params("parallel", "parallel"),
        name="diff_attn",
    )(q3, k3, vt4, lam_params, gain_col)
    return out.reshape(batch * seq, D_MODEL)


ML_SCANS = 2 * ML_HEADS


def _mlstm_proj_kernel(x_ref, g_ref, wqt_ref, wk_ref, wvt_ref, wo_ref, wgt_ref, bgt_ref,
                       qt_ref, k_ref, vt_ref, o_ref, gr_ref, uc_ref):
    tm = x_ref.shape[0]
    xn = _rms(x_ref[...], g_ref[...]).astype(BF16)
    qt = _dot_nt(wqt_ref[...], xn).astype(BF16)
    k_ref[...] = (_dot(xn, wk_ref[...]) * ML_QK_DIM ** -0.5).astype(BF16)
    vt = _dot_nt(wvt_ref[...], xn).astype(BF16)
    o_ref[...] = _dot(xn, wo_ref[...])

    gates = _dot_nt(wgt_ref[...], xn) + bgt_ref[...]
    fg = gates[:ML_SCANS]
    log_f = jnp.minimum(fg, 0.0) - jnp.log1p(jnp.exp(-jnp.abs(fg)))
    pos = lax.broadcasted_iota(jnp.int32, (1, tm), 1) % ML_CHUNK
    pre = suf = log_f
    shift = 1
    while shift < ML_CHUNK:
        pre = pre + jnp.where(pos >= shift, pltpu.roll(pre, shift, 1), 0.0)
        suf = suf + jnp.where(pos < ML_CHUNK - shift, pltpu.roll(suf, tm - shift, 1), 0.0)
        shift *= 2
    is_fwd = lax.broadcasted_iota(jnp.int32, (ML_SCANS, 1), 0) < ML_HEADS
    b = jnp.where(is_fwd, pre, suf)
    u = gates[ML_SCANS:] - b
    gr = jnp.concatenate([b, u], axis=0)
    uc_ref[...] = jnp.concatenate([u, jnp.zeros((LANES - ML_SCANS, tm), F32)], axis=0).T
    for c in range(tm // ML_CHUNK):
        cols = slice(c * ML_CHUNK, (c + 1) * ML_CHUNK)
        qt_ref[0, c] = qt[:, cols]
        vt_ref[0, c] = vt[:, cols]
        gr_ref[0, c] = gr[:, cols]


def _mlstm_proj(x2d, gain, wqt, wk, wvt, wo, wgt, bgt):
    t = x2d.shape[0]
    tm = TM_PROJ
    cpt = tm // ML_CHUNK
    row = lambda n: pl.BlockSpec((tm, n), lambda i: (i, 0))
    chunks = lambda n: pl.BlockSpec((1, cpt, n, ML_CHUNK), lambda i: (i, 0, 0, 0))
    chunk_shape = lambda n, dt: jax.ShapeDtypeStruct((t // tm, cpt, n, ML_CHUNK), dt)
    return pl.pallas_call(
        _mlstm_proj_kernel,
        grid=(t // tm,),
        in_specs=[row(D_MODEL), _const_spec((1, D_MODEL)),
                  _const_spec((ML_QK_COLS, D_MODEL)), _const_spec((D_MODEL, ML_QK_COLS)),
                  _const_spec((D_MODEL, D_MODEL)), _const_spec((D_MODEL, D_MODEL)),
                  _const_spec((ML_GATE_COLS, D_MODEL)), _const_spec((ML_GATE_COLS, 1))],
        out_specs=[chunks(ML_QK_COLS), row(ML_QK_COLS), chunks(D_MODEL), row(D_MODEL),
                   chunks(ML_GATE_COLS), row(LANES)],
        out_shape=[chunk_shape(ML_QK_COLS, BF16),
                   jax.ShapeDtypeStruct((t, ML_QK_COLS), BF16),
                   chunk_shape(D_MODEL, BF16),
                   jax.ShapeDtypeStruct((t, D_MODEL), F32),
                   chunk_shape(ML_GATE_COLS, F32),
                   jax.ShapeDtypeStruct((t, LANES), F32)],
        compiler_params=_params("parallel"),
        name="mlstm_proj",
    )(x2d, gain, wqt, wk, wvt, wo, wgt, bgt)


ML_STATE_ROWS = ML_V_DIM + 16


def _mlstm_chunk(qt, k, vt, gr, uc, scan, state_ref, m_ref, reverse):
    L = ML_CHUNK
    b_row = gr[scan:scan + 1]
    u_row = gr[ML_SCANS + scan:ML_SCANS + scan + 1]
    u_col = uc[:, scan:scan + 1]
    b_last = b_row[:, 0:1] if reverse else b_row[:, L - 1:L]
    s_idx = lax.broadcasted_iota(jnp.int32, (L, L), 0)
    j_idx = lax.broadcasted_iota(jnp.int32, (L, L), 1)
    keep = (s_idx >= j_idx) if reverse else (s_idx <= j_idx)
    m_old = m_ref[0:1, 0:1]

    dlog = jnp.where(keep, u_col + b_row, -jnp.inf)
    g = b_row + m_old
    m_row = jnp.maximum(g, jnp.max(dlog, axis=0, keepdims=True))
    w_intra = jnp.exp(dlog - m_row)
    w_inter = jnp.exp(g - m_row)

    st = _dot(k, qt) * w_intra
    state = state_ref[...]
    inter = _dot(state.astype(BF16), qt)
    num = w_inter * inter[:ML_V_DIM] + _dot(vt, st.astype(BF16))
    den = w_inter * inter[ML_V_DIM:ML_V_DIM + 1] + jnp.sum(st, axis=0, keepdims=True)
    ht = num / jnp.maximum(jnp.abs(den), jnp.exp(-m_row))

    w_state_log = b_last + u_row
    m_new = jnp.maximum(b_last + m_old, jnp.max(w_state_log, axis=1, keepdims=True))
    w_state = jnp.exp(w_state_log - m_new)
    decay = jnp.exp(b_last + m_old - m_new)
    vt_aug = jnp.concatenate(
        [vt.astype(F32), jnp.ones((ML_STATE_ROWS - ML_V_DIM, L), F32)], axis=0)
    state_ref[...] = decay * state + _dot((vt_aug * w_state).astype(BF16), k)
    m_ref[...] = jnp.broadcast_to(m_new, m_ref.shape)
    return ht.T


def _mlstm_scan_kernel(qtf_ref, kf_ref, vtf_ref, grf_ref, ucf_ref,
                       qtb_ref, kb_ref, vtb_ref, grb_ref, ucb_ref,
                       hf_ref, hb_ref, state_ref, m_ref):
    @pl.when(pl.program_id(1) == 0)
    def _():
        state_ref[...] = jnp.zeros_like(state_ref)
        m_ref[...] = jnp.zeros_like(m_ref)

    n_chunks = kf_ref.shape[1] // ML_CHUNK
    fwd_refs = (qtf_ref, kf_ref, vtf_ref, grf_ref, ucf_ref, hf_ref)
    bwd_refs = (qtb_ref, kb_ref, vtb_ref, grb_ref, ucb_ref, hb_ref)

    def step(i, carry):
        for head in range(ML_HEADS):
            for reverse in (False, True):
                qt_ref, k_ref, vt_ref, gr_ref, uc_ref, h_ref = bwd_refs if reverse else fwd_refs
                c = n_chunks - 1 - i if reverse else i
                rows = pl.ds(pl.multiple_of(c * ML_CHUNK, ML_CHUNK), ML_CHUNK)
                qk_cols = slice(head * ML_QK_DIM, (head + 1) * ML_QK_DIM)
                v_cols = slice(head * ML_V_DIM, (head + 1) * ML_V_DIM)
                scan = int(reverse) * ML_HEADS + head
                h_ref[0, rows, v_cols] = _mlstm_chunk(
                    qt_ref[0, c, qk_cols, :], k_ref[0, rows, qk_cols], vt_ref[0, c, v_cols, :],
                    gr_ref[0, c], uc_ref[0, rows, :], scan,
                    state_ref.at[scan], m_ref.at[scan], reverse)
        return carry

    lax.fori_loop(0, n_chunks, step, 0)


def _mlstm_scan(qt, k, vt, gr, uc, batch, seq):
    tm = TM_ML
    cpt = tm // ML_CHUNK
    nt = seq // tm
    n_chunks = seq // ML_CHUNK
    qt4 = qt.reshape(batch, n_chunks, ML_QK_COLS, ML_CHUNK)
    k3 = k.reshape(batch, seq, ML_QK_COLS)
    vt4 = vt.reshape(batch, n_chunks, D_MODEL, ML_CHUNK)
    gr4 = gr.reshape(batch, n_chunks, ML_GATE_COLS, ML_CHUNK)
    uc3 = uc.reshape(batch, seq, LANES)

    def specs(pos):
        return [pl.BlockSpec((1, cpt, ML_QK_COLS, ML_CHUNK), lambda b, i: (b, pos(i), 0, 0)),
                pl.BlockSpec((1, tm, ML_QK_COLS), lambda b, i: (b, pos(i), 0)),
                pl.BlockSpec((1, cpt, D_MODEL, ML_CHUNK), lambda b, i: (b, pos(i), 0, 0)),
                pl.BlockSpec((1, cpt, ML_GATE_COLS, ML_CHUNK), lambda b, i: (b, pos(i), 0, 0)),
                pl.BlockSpec((1, tm, LANES), lambda b, i: (b, pos(i), 0))]

    fwd = lambda i: i
    bwd = lambda i: nt - 1 - i
    h_shape = jax.ShapeDtypeStruct((batch, seq, D_MODEL), F32)
    return pl.pallas_call(
        _mlstm_scan_kernel,
        grid=(batch, nt),
        in_specs=specs(fwd) + specs(bwd),
        out_specs=[pl.BlockSpec((1, tm, D_MODEL), lambda b, i: (b, fwd(i), 0)),
                   pl.BlockSpec((1, tm, D_MODEL), lambda b, i: (b, bwd(i), 0))],
        out_shape=[h_shape, h_shape],
        scratch_shapes=[pltpu.VMEM((ML_SCANS, ML_STATE_ROWS, ML_QK_DIM), F32),
                        pltpu.VMEM((ML_SCANS, 8, LANES), F32)],
        compiler_params=_params("parallel", "arbitrary"),
        name="mlstm_scan",
    )(qt4, k3, vt4, gr4, uc3, qt4, k3, vt4, gr4, uc3)


def _ffn_tail(x, h, g1, g2, g3, wgu_ref, wd_ref, out_ref):
    x1 = x + _rms(h, g1)
    xn = _rms(x1, g2).astype(BF16)
    y = None
    for lo, hi in FFN_SPLITS:
        gate = _dot(xn, wgu_ref[:, lo:hi])
        up = _dot(xn, wgu_ref[:, FFN_HIDDEN + lo:FFN_HIDDEN + hi])
        act = (gate * jax.nn.sigmoid(gate) * up).astype(BF16)
        part = _dot(act, wd_ref[lo:hi, :])
        y = part if y is None else y + part
    out_ref[...] = x1 + _rms(y, g3)


def _attn_post_kernel(x_ref, a_ref, wo_ref, g_ref, wgu_ref, wd_ref, out_ref):
    h = _dot(a_ref[...], wo_ref[...])
    _ffn_tail(x_ref[...], h, g_ref[0:1], g_ref[1:2], g_ref[2:3], wgu_ref, wd_ref, out_ref)


def _mlstm_post_kernel(x_ref, hf_ref, hb_ref, o_ref, hg_ref, wo_ref, g_ref, wgu_ref, wd_ref,
                       out_ref):
    hsum = hf_ref[...] + hb_ref[...]
    heads = []
    for hd in range(ML_HEADS):
        cols = slice(hd * ML_V_DIM, (hd + 1) * ML_V_DIM)
        heads.append(_rms(hsum[:, cols], hg_ref[:, cols]))
    a = (jnp.concatenate(heads, axis=1) * jax.nn.sigmoid(o_ref[...])).astype(BF16)
    h = _dot(a, wo_ref[...])
    _ffn_tail(x_ref[...], h, g_ref[0:1], g_ref[1:2], g_ref[2:3], wgu_ref, wd_ref, out_ref)


def _pool_post_kernel(x_ref, prev_ref, next_ref, g0_ref, wp_ref, bp_ref, sp_ref, g_ref,
                      wgu_ref, wd_ref, out_ref, ext_ref, *, tiles_per_seq):
    tm = x_ref.shape[0]
    hl = POOL_HALO
    pos = pl.program_id(0) % tiles_per_seq
    x = x_ref[...]
    g0 = g0_ref[...]
    hn = _rms(x, g0)
    ext_ref[0:hl, :] = jnp.where(pos > 0, _rms(prev_ref[...], g0), 0.0)
    ext_ref[hl:hl + tm, :] = hn
    ext_ref[hl + tm:, :] = jnp.where(pos < tiles_per_seq - 1, _rms(next_ref[...], g0), 0.0)
    t = pos * tm + lax.broadcasted_iota(jnp.int32, (tm, 1), 0)
    seq = tiles_per_seq * tm
    parts = []
    for g, w in enumerate(POOL_WINDOWS):
        cols = slice(g * POOL_GROUP_DIM, (g + 1) * POOL_GROUP_DIM)
        tot = ext_ref[hl - w // 2:hl - w // 2 + tm, cols]
        for off in range(-w // 2 + 1, w // 2):
            tot = tot + ext_ref[hl + off:hl + off + tm, cols]
        cnt = (jnp.minimum(t + w // 2, seq) - jnp.maximum(t - w // 2, 0)).astype(F32)
        d = (tot / cnt - hn[:, cols]).astype(BF16)
        parts.append(_dot(d, wp_ref[g]))
    h = (jnp.concatenate(parts, axis=1) + bp_ref[...]) * sp_ref[...]
    _ffn_tail(x, h, g_ref[0:1], g_ref[1:2], g_ref[2:3], wgu_ref, wd_ref, out_ref)


def _ffn_specs():
    return [_const_spec((3, D_MODEL)), _const_spec((D_MODEL, 2 * FFN_HIDDEN)),
            _const_spec((FFN_HIDDEN, D_MODEL))]


def _attn_post(x2d, a, wo, gains, wgu, wd):
    t = x2d.shape[0]
    row = pl.BlockSpec((TM_FFN, D_MODEL), lambda i: (i, 0))
    return pl.pallas_call(
        _attn_post_kernel,
        grid=(t // TM_FFN,),
        in_specs=[row, row, _const_spec((D_MODEL, D_MODEL))] + _ffn_specs(),
        out_specs=row,
        out_shape=jax.ShapeDtypeStruct((t, D_MODEL), F32),
        compiler_params=_params("parallel"),
        name="attn_post_ffn",
    )(x2d, a, wo, gains, wgu, wd)


def _mlstm_post(x2d, hf, hb, o, head_gain, wo, gains, wgu, wd):
    t = x2d.shape[0]
    row = pl.BlockSpec((TM_FFN, D_MODEL), lambda i: (i, 0))
    return pl.pallas_call(
        _mlstm_post_kernel,
        grid=(t // TM_FFN,),
        in_specs=[row, row, row, row, _const_spec((1, D_MODEL)),
                  _const_spec((D_MODEL, D_MODEL))] + _ffn_specs(),
        out_specs=row,
        out_shape=jax.ShapeDtypeStruct((t, D_MODEL), F32),
        compiler_params=_params("parallel"),
        name="mlstm_post_ffn",
    )(x2d, hf, hb, o, head_gain, wo, gains, wgu, wd)


def _pool_post(x2d, g0, wp, bp, sp, gains, wgu, wd, seq):
    t = x2d.shape[0]
    tm = TM_FFN
    tps = seq // tm
    hpt = tm // POOL_HALO
    n_halo = t // POOL_HALO
    row = pl.BlockSpec((tm, D_MODEL), lambda i: (i, 0))
    prev = pl.BlockSpec((POOL_HALO, D_MODEL), lambda i: (jnp.maximum(i * hpt - 1, 0), 0))
    nxt = pl.BlockSpec((POOL_HALO, D_MODEL),
                       lambda i: (jnp.minimum((i + 1) * hpt, n_halo - 1), 0))
    kern = functools.partial(_pool_post_kernel, tiles_per_seq=tps)
    return pl.pallas_call(
        kern,
        grid=(t // tm,),
        in_specs=[row, prev, nxt, _const_spec((1, D_MODEL)),
                  _const_spec((POOL_GROUPS, POOL_GROUP_DIM, POOL_GROUP_DIM)),
                  _const_spec((1, D_MODEL)), _const_spec((1, D_MODEL))] + _ffn_specs(),
        out_specs=row,
        out_shape=jax.ShapeDtypeStruct((t, D_MODEL), F32),
        scratch_shapes=[pltpu.VMEM((tm + 2 * POOL_HALO, D_MODEL), F32)],
        compiler_params=_params("parallel"),
        name="pool_post_ffn",
    )(x2d, x2d, x2d, g0, wp, bp, sp, gains, wgu, wd)


def _rope_tables(seq):
    inv = ROPE_THETA ** (-jnp.arange(0, ROPE_DIM, 2, dtype=F32) / ROPE_DIM)
    ang = jnp.arange(seq, dtype=F32)[:, None] * inv[None, :]
    cos, sin = jnp.cos(ang), jnp.sin(ang)
    pad = DA_QK_DIM - ROPE_DIM
    cos_c = jnp.concatenate([cos, cos, jnp.ones((seq, pad), F32)], axis=1)
    sin_c = jnp.concatenate([-sin, sin, jnp.zeros((seq, pad), F32)], axis=1)
    return jnp.tile(cos_c, (1, 2)), jnp.tile(sin_c, (1, 2))


def _lambda_init(layer_idx):
    return 0.8 - 0.6 * math.exp(-0.3 * layer_idx)


def _trunk(x, w):
    batch, seq, _ = x.shape
    x2d = x.reshape(batch * seq, D_MODEL)
    cos_t, sin_t = _rope_tables(seq)
    for i in range(DEPTH):
        kind, slot = i % N_MIXERS, i // N_MIXERS
        g0 = w["norm_gain"][i, 0:1]
        gains = w["norm_gain"][i, 1:4]
        wgu, wd = w["w_ffn_gate_up"][i], w["w_ffn_down"][i]
        if kind == 0:
            q, k, vt = _attn_proj(x2d, g0, w["wq"][slot], w["wk"][slot], w["wvt"][slot],
                                  cos_t, sin_t, seq)
            a = _diff_attn(q, k, vt, w["attn_lambda"][slot], w["subln_col"][slot],
                           batch, seq, _lambda_init(i))
            x2d = _attn_post(x2d, a, w["w_attn_out"][slot], gains, wgu, wd)
        elif kind == 1:
            x2d = _pool_post(x2d, g0, w["w_pool"][slot], w["b_pool"][slot],
                             w["pool_scale"][slot], gains, wgu, wd, seq)
        else:
            qt, k, vt, o, gr, uc = _mlstm_proj(
                x2d, g0, w["ml_wqt"][slot], w["ml_wk"][slot], w["ml_wvt"][slot],
                w["ml_wo"][slot], w["ml_wgt"][slot], w["ml_bgt"][slot])
            hf, hb = _mlstm_scan(qt, k, vt, gr, uc, batch, seq)
            x2d = _mlstm_post(x2d, hf.reshape(-1, D_MODEL), hb.reshape(-1, D_MODEL), o,
                              w["ml_head_gain"][slot], w["w_mlstm_out"][slot], gains, wgu, wd)
    return x2d.reshape(batch, seq, D_MODEL)


def _prepare_weights(norm_gain, w_attn_in, attn_lambda, attn_subln_gain, w_attn_out, w_pool,
                     b_pool, pool_scale, w_mlstm_in, b_mlstm_gate, mlstm_head_gain,
                     w_mlstm_out, w_ffn_gate_up, w_ffn_down):
    d, qk = D_MODEL, ML_QK_COLS
    swap = lambda a: jnp.swapaxes(a, -1, -2)
    wm = w_mlstm_in
    wg = wm[:, :, 2 * qk + 2 * d:]
    nh = ML_HEADS
    gate_order = jnp.array([kind * nh + h for kind in (1, 3, 0, 2) for h in range(nh)])
    return {
        "norm_gain": norm_gain,
        "wq": w_attn_in[:, :, :d].astype(BF16),
        "wk": w_attn_in[:, :, d:2 * d].astype(BF16),
        "wvt": swap(w_attn_in[:, :, 2 * d:]).astype(BF16),
        "attn_lambda": attn_lambda,
        "subln_col": attn_subln_gain[:, :, None],
        "w_attn_out": w_attn_out.astype(BF16),
        "w_pool": w_pool.astype(BF16),
        "b_pool": b_pool[:, None, :],
        "pool_scale": pool_scale[:, None, :],
        "ml_wqt": swap(wm[:, :, :qk]).astype(BF16),
        "ml_wk": wm[:, :, qk:2 * qk].astype(BF16),
        "ml_wvt": swap(wm[:, :, 2 * qk:2 * qk + d]).astype(BF16),
        "ml_wo": wm[:, :, 2 * qk + d:2 * qk + 2 * d].astype(BF16),
        "ml_wgt": swap(wg[:, :, gate_order]).astype(BF16),
        "ml_bgt": b_mlstm_gate[:, gate_order, None],
        "ml_head_gain": mlstm_head_gain[:, None, :],
        "w_mlstm_out": w_mlstm_out.astype(BF16),
        "w_ffn_gate_up": w_ffn_gate_up.astype(BF16),
        "w_ffn_down": w_ffn_down.astype(BF16),
    }


def kernel(x_prompt, x_sample, norm_gain, w_attn_in, attn_lambda, attn_subln_gain, w_attn_out,
           w_pool, b_pool, pool_scale, w_mlstm_in, b_mlstm_gate, mlstm_head_gain, w_mlstm_out,
           w_ffn_gate_up, w_ffn_down):
    w = _prepare_weights(norm_gain, w_attn_in, attn_lambda, attn_subln_gain, w_attn_out,
                         w_pool, b_pool, pool_scale, w_mlstm_in, b_mlstm_gate,
                         mlstm_head_gain, w_mlstm_out, w_ffn_gate_up, w_ffn_down)
    return _trunk(x_prompt, w), _trunk(x_sample, w)
```

```python
import functools
import math

import jax
import jax.numpy as jnp
from jax import lax
from jax.experimental import pallas as pl
from jax.experimental.pallas import tpu as pltpu

D_MODEL = 1024
DEPTH = 4
N_MIXERS = 3
DA_HEADS = 8
DA_QK_DIM = 64
DA_V_DIM = 128
ROPE_DIM = 16
ROPE_THETA = 500000.0
POOL_GROUPS = 4
POOL_WINDOWS = (2, 4, 8, 16)
POOL_GROUP_DIM = D_MODEL // POOL_GROUPS
POOL_HALO = 8
ML_HEADS = 4
ML_QK_DIM = 128
ML_V_DIM = 256
ML_CHUNK = 128
ML_QK_COLS = ML_HEADS * ML_QK_DIM
ML_GATE_COLS = 4 * ML_HEADS
FFN_HIDDEN = 2816
NORM_EPS = 1e-6
LOG2E = math.log2(math.e)

LANES = 128
VMEM_LIMIT = 56 * 1024 * 1024
TM_PROJ = 512
TM_FFN = 512
FFN_SUB_ROWS = 256
FFN_SPLITS = ((0, 1536), (1536, FFN_HIDDEN))
TQ_ATTN = 256
TM_ML = 512

BF16 = jnp.bfloat16
F32 = jnp.float32


def _params(*sem):
    return pltpu.CompilerParams(dimension_semantics=sem, vmem_limit_bytes=VMEM_LIMIT)


def _const_spec(shape):
    nd = len(shape)
    return pl.BlockSpec(shape, lambda *_: (0,) * nd, pipeline_mode=pl.Buffered(1))


def _rms(x, gain):
    ms = jnp.mean(x * x, axis=-1, keepdims=True)
    return x * lax.rsqrt(ms + NORM_EPS) * gain


def _dot(a, b):
    return jnp.dot(a, b, preferred_element_type=F32)


def _dot_nt(a, b):
    return lax.dot_general(a, b, (((1,), (1,)), ((), ())), preferred_element_type=F32)


def _attn_proj_kernel(x_ref, g_ref, wq_ref, wk_ref, wvt_ref, cos_ref, sin_ref,
                      q_ref, k_ref, vt_ref):
    xn = _rms(x_ref[...], g_ref[...]).astype(BF16)
    cos = cos_ref[...]
    sin = sin_ref[...]
    lane = lax.broadcasted_iota(jnp.int32, (1, LANES), 1) % DA_QK_DIM
    first_half = lane < ROPE_DIM // 2

    def rope(y, scale):
        outs = []
        for h in range(DA_HEADS):
            slab = y[:, h * LANES:(h + 1) * LANES]
            partner = jnp.where(first_half,
                                pltpu.roll(slab, LANES - ROPE_DIM // 2, 1),
                                pltpu.roll(slab, ROPE_DIM // 2, 1))
            outs.append(((slab * cos + partner * sin) * scale).astype(BF16))
        return jnp.concatenate(outs, axis=1)

    q_ref[...] = rope(_dot(xn, wq_ref[...]), DA_QK_DIM ** -0.5 * LOG2E)
    k_ref[...] = rope(_dot(xn, wk_ref[...]), 1.0)
    vt_ref[0] = _dot_nt(wvt_ref[...], xn).astype(BF16)


def _attn_proj(x2d, gain, wq, wk, wvt, cos_t, sin_t, seq):
    t = x2d.shape[0]
    tm = TM_PROJ
    nps = seq // tm
    row = pl.BlockSpec((tm, D_MODEL), lambda i: (i, 0))
    tab = pl.BlockSpec((tm, LANES), lambda i: (i % nps, 0))
    return pl.pallas_call(
        _attn_proj_kernel,
        grid=(t // tm,),
        in_specs=[row, _const_spec((1, D_MODEL)), _const_spec((D_MODEL, D_MODEL)),
                  _const_spec((D_MODEL, D_MODEL)), _const_spec((D_MODEL, D_MODEL)), tab, tab],
        out_specs=[row, row, pl.BlockSpec((1, D_MODEL, tm), lambda i: (i, 0, 0))],
        out_shape=[jax.ShapeDtypeStruct((t, D_MODEL), BF16),
                   jax.ShapeDtypeStruct((t, D_MODEL), BF16),
                   jax.ShapeDtypeStruct((t // tm, D_MODEL, tm), BF16)],
        compiler_params=_params("parallel"),
        name="attn_proj",
    )(x2d, gain, wq, wk, wvt, cos_t, sin_t)


DV_AUG = DA_V_DIM + 16


def _diff_attn_kernel(q_ref, k_ref, vt_ref, lam_ref, gain_ref, o_ref, acc_ref, fin_ref,
                      sa_ref, sb_ref, *, n_kv, tk, tq, unroll, q_unroll, lambda_init):
    n_q = q_ref.shape[1] // tq
    n_pairs = n_kv // 2
    lane = lax.broadcasted_iota(jnp.int32, (1, LANES), 1)
    lp = lam_ref[...]
    lam = (jnp.exp(jnp.sum(lp[0:1] * lp[1:2], keepdims=True))
           - jnp.exp(jnp.sum(lp[2:3] * lp[3:4], keepdims=True)) + lambda_init)
    out_gain = gain_ref[...] * (1.0 - lambda_init)
    ones_rows = jnp.ones((DV_AUG - DA_V_DIM, tk), BF16)
    neg = jnp.full((1, tq), -jnp.inf, F32)

    def masked_q(qi):
        q = q_ref[0, pl.ds(pl.multiple_of(qi * tq, tq), tq), :]
        zero = jnp.zeros_like(q)
        return jnp.where(lane < DA_QK_DIM, q, zero), jnp.where(lane >= DA_QK_DIM, q, zero)

    def scores(qm, j, dst_ref):
        kc = k_ref[0, pl.ds(pl.multiple_of(j * tk, tk), tk), :]
        cmax = []
        for c in range(2):
            s = _dot_nt(kc, qm[c])
            dst_ref[c] = s
            cmax.append(jnp.max(s, axis=0, keepdims=True))
        return tuple(cmax)

    def consume(j, src_ref, cmax, m):
        vt = jnp.concatenate([vt_ref[0, j], ones_rows], axis=0)
        m_out = []
        for c in range(2):
            m_new = jnp.maximum(m[c], cmax[c])
            alpha = jnp.exp2(m[c] - m_new)
            p = jnp.exp2(src_ref[c] - m_new).astype(BF16)
            acc_ref[c] = alpha * acc_ref[c] + _dot(vt, p)
            m_out.append(m_new)
        return tuple(m_out)

    def finalize(qi):
        a1, a2 = fin_ref[0], fin_ref[1]
        o = (a1[:DA_V_DIM] / a1[DA_V_DIM:DA_V_DIM + 1]
             - lam * (a2[:DA_V_DIM] / a2[DA_V_DIM:DA_V_DIM + 1]))
        ms = jnp.mean(o * o, axis=0, keepdims=True)
        y = o * lax.rsqrt(ms + NORM_EPS) * out_gain
        start = qi * tq if isinstance(qi, int) else pl.multiple_of(qi * tq, tq)
        o_ref[0, pl.ds(start, tq), :] = y.T.astype(BF16)

    def q_tile(qi, cmax_a):
        qm = masked_q(qi)
        qm_next = masked_q(jnp.minimum(qi + 1, n_q - 1))
        finalize(jnp.maximum(qi - 1, 0))

        def pair(i, carry):
            m, cmax_a = carry
            j = 2 * i
            cmax_b = scores(qm, j + 1, sb_ref)
            m = consume(j, sa_ref, cmax_a, m)
            cmax_a = scores(qm, j + 2, sa_ref)
            return consume(j + 1, sb_ref, cmax_b, m), cmax_a

        m, cmax_a = lax.fori_loop(0, n_pairs - 1, pair, ((neg, neg), cmax_a), unroll=unroll)
        cmax_b = scores(qm, n_kv - 1, sb_ref)
        m = consume(n_kv - 2, sa_ref, cmax_a, m)
        cmax_a = scores(qm_next, 0, sa_ref)
        consume(n_kv - 1, sb_ref, cmax_b, m)
        fin_ref[...] = acc_ref[...]
        return cmax_a

    acc_ref[...] = jnp.zeros_like(acc_ref)
    fin_ref[...] = jnp.ones_like(fin_ref)
    cmax_0 = scores(masked_q(0), 0, sa_ref)
    lax.fori_loop(0, n_q, q_tile, cmax_0, unroll=q_unroll)
    finalize(n_q - 1)


def _diff_attn(q, k, vt, lam_params, gain_col, batch, seq, lambda_init):
    tq, tk = TQ_ATTN, TM_PROJ
    n_kv = seq // tk
    assert n_kv % 2 == 0 and seq % tq == 0
    n_loop = n_kv // 2 - 1
    unroll = True if n_loop <= 1 else (5 if n_loop % 5 == 0 else 1)
    q3 = q.reshape(batch, seq, D_MODEL)
    k3 = k.reshape(batch, seq, D_MODEL)
    vt4 = vt.reshape(batch, n_kv, D_MODEL, tk)
    q_unroll = 4 if n_loop <= 1 and (seq // tq) % 4 == 0 else 1
    kern = functools.partial(_diff_attn_kernel, n_kv=n_kv, tk=tk, tq=tq, unroll=unroll,
                             q_unroll=q_unroll, lambda_init=lambda_init)
    head = pl.BlockSpec((1, seq, LANES), lambda b, h: (b, 0, h))
    out = pl.pallas_call(
        kern,
        grid=(batch, DA_HEADS),
        in_specs=[head, head,
                  pl.BlockSpec((1, n_kv, DA_V_DIM, tk), lambda b, h: (b, 0, h, 0)),
                  pl.BlockSpec((4, DA_QK_DIM), lambda b, h: (0, 0)),
                  pl.BlockSpec((DA_V_DIM, 1), lambda b, h: (0, 0))],
        out_specs=head,
        out_shape=jax.ShapeDtypeStruct((batch, seq, D_MODEL), BF16),
        scratch_shapes=[pltpu.VMEM((2, DV_AUG, tq), F32), pltpu.VMEM((2, DV_AUG, tq), F32),
                        pltpu.VMEM((2, tk, tq), F32), pltpu.VMEM((2, tk, tq), F32)],
        compiler_params=_params("parallel", "parallel"),
        name="diff_attn",
    )(q3, k3, vt4, lam_params, gain_col)
    return out.reshape(batch * seq, D_MODEL)


ML_SCANS = 2 * ML_HEADS


def _mlstm_proj_kernel(x_ref, g_ref, wqt_ref, wk_ref, wvt_ref, wo_ref, wgt_ref, bgt_ref,
                       qt_ref, k_ref, vt_ref, o_ref, gr_ref, uc_ref):
    tm = x_ref.shape[0]
    xn = _rms(x_ref[...], g_ref[...]).astype(BF16)

    gates = _dot_nt(wgt_ref[...], xn) + bgt_ref[...]
    fg = gates[:ML_SCANS]
    log_f = jnp.minimum(fg, 0.0) - jnp.log1p(jnp.exp(-jnp.abs(fg)))
    pos = lax.broadcasted_iota(jnp.int32, (1, tm), 1) % ML_CHUNK
    pre = suf = log_f
    shift = 1
    while shift < ML_CHUNK:
        pre = pre + jnp.where(pos >= shift, pltpu.roll(pre, shift, 1), 0.0)
        suf = suf + jnp.where(pos < ML_CHUNK - shift, pltpu.roll(suf, tm - shift, 1), 0.0)
        shift *= 2
    is_fwd = lax.broadcasted_iota(jnp.int32, (ML_SCANS, 1), 0) < ML_HEADS
    b = jnp.where(is_fwd, pre, suf)
    u = gates[ML_SCANS:] - b
    gr = jnp.concatenate([b, u], axis=0)
    uc_ref[...] = jnp.concatenate([u, jnp.zeros((LANES - ML_SCANS, tm), F32)], axis=0).T

    qt = _dot_nt(wqt_ref[...], xn).astype(BF16)
    k_ref[...] = (_dot(xn, wk_ref[...]) * ML_QK_DIM ** -0.5).astype(BF16)
    vt = _dot_nt(wvt_ref[...], xn).astype(BF16)
    o_ref[...] = _dot(xn, wo_ref[...])
    for c in range(tm // ML_CHUNK):
        cols = slice(c * ML_CHUNK, (c + 1) * ML_CHUNK)
        qt_ref[0, c] = qt[:, cols]
        vt_ref[0, c] = vt[:, cols]
        gr_ref[0, c] = gr[:, cols]


def _mlstm_proj(x2d, gain, wqt, wk, wvt, wo, wgt, bgt):
    t = x2d.shape[0]
    tm = TM_PROJ
    cpt = tm // ML_CHUNK
    row = lambda n: pl.BlockSpec((tm, n), lambda i: (i, 0))
    chunks = lambda n: pl.BlockSpec((1, cpt, n, ML_CHUNK), lambda i: (i, 0, 0, 0))
    chunk_shape = lambda n, dt: jax.ShapeDtypeStruct((t // tm, cpt, n, ML_CHUNK), dt)
    return pl.pallas_call(
        _mlstm_proj_kernel,
        grid=(t // tm,),
        in_specs=[row(D_MODEL), _const_spec((1, D_MODEL)),
                  _const_spec((ML_QK_COLS, D_MODEL)), _const_spec((D_MODEL, ML_QK_COLS)),
                  _const_spec((D_MODEL, D_MODEL)), _const_spec((D_MODEL, D_MODEL)),
                  _const_spec((ML_GATE_COLS, D_MODEL)), _const_spec((ML_GATE_COLS, 1))],
        out_specs=[chunks(ML_QK_COLS), row(ML_QK_COLS), chunks(D_MODEL), row(D_MODEL),
                   chunks(ML_GATE_COLS), row(LANES)],
        out_shape=[chunk_shape(ML_QK_COLS, BF16),
                   jax.ShapeDtypeStruct((t, ML_QK_COLS), BF16),
                   chunk_shape(D_MODEL, BF16),
                   jax.ShapeDtypeStruct((t, D_MODEL), F32),
                   chunk_shape(ML_GATE_COLS, F32),
                   jax.ShapeDtypeStruct((t, LANES), F32)],
        compiler_params=_params("parallel"),
        name="mlstm_proj",
    )(x2d, gain, wqt, wk, wvt, wo, wgt, bgt)


ML_STATE_ROWS = ML_V_DIM + 16


def _mlstm_chunk(qt, k, vt, gr, uc, scan, state_ref, m_ref, reverse):
    L = ML_CHUNK
    b_row = gr[scan:scan + 1]
    u_row = gr[ML_SCANS + scan:ML_SCANS + scan + 1]
    u_col = uc[:, scan:scan + 1]
    b_last = b_row[:, 0:1] if reverse else b_row[:, L - 1:L]
    s_idx = lax.broadcasted_iota(jnp.int32, (L, L), 0)
    j_idx = lax.broadcasted_iota(jnp.int32, (L, L), 1)
    keep = (s_idx >= j_idx) if reverse else (s_idx <= j_idx)
    m_old = m_ref[0:1, 0:1]

    dlog = jnp.where(keep, u_col + b_row, -jnp.inf)
    g = b_row + m_old
    m_row = jnp.maximum(g, jnp.max(dlog, axis=0, keepdims=True))
    w_intra = jnp.exp(dlog - m_row)
    w_inter = jnp.exp(g - m_row)

    st = _dot(k, qt) * w_intra
    state = state_ref[...]
    inter = _dot(state.astype(BF16), qt)
    num = w_inter * inter[:ML_V_DIM] + _dot(vt, st.astype(BF16))
    den = w_inter * inter[ML_V_DIM:ML_V_DIM + 1] + jnp.sum(st, axis=0, keepdims=True)
    ht = num / jnp.maximum(jnp.abs(den), jnp.exp(-m_row))

    w_state_log = b_last + u_row
    m_new = jnp.maximum(b_last + m_old, jnp.max(w_state_log, axis=1, keepdims=True))
    w_state = jnp.exp(w_state_log - m_new)
    decay = jnp.exp(b_last + m_old - m_new)
    vt_aug = jnp.concatenate(
        [vt.astype(F32), jnp.ones((ML_STATE_ROWS - ML_V_DIM, L), F32)], axis=0)
    state_ref[...] = decay * state + _dot((vt_aug * w_state).astype(BF16), k)
    m_ref[...] = jnp.broadcast_to(m_new, m_ref.shape)
    return ht.T


def _mlstm_scan_kernel(qtf_ref, kf_ref, vtf_ref, grf_ref, ucf_ref,
                       qtb_ref, kb_ref, vtb_ref, grb_ref, ucb_ref,
                       hf_ref, hb_ref, state_ref, m_ref):
    @pl.when(pl.program_id(1) == 0)
    def _():
        state_ref[...] = jnp.zeros_like(state_ref)
        m_ref[...] = jnp.zeros_like(m_ref)

    n_chunks = kf_ref.shape[1] // ML_CHUNK
    fwd_refs = (qtf_ref, kf_ref, vtf_ref, grf_ref, ucf_ref, hf_ref)
    bwd_refs = (qtb_ref, kb_ref, vtb_ref, grb_ref, ucb_ref, hb_ref)

    def step(i, carry):
        for head in range(ML_HEADS):
            for reverse in (False, True):
                qt_ref, k_ref, vt_ref, gr_ref, uc_ref, h_ref = bwd_refs if reverse else fwd_refs
                c = n_chunks - 1 - i if reverse else i
                rows = pl.ds(pl.multiple_of(c * ML_CHUNK, ML_CHUNK), ML_CHUNK)
                qk_cols = slice(head * ML_QK_DIM, (head + 1) * ML_QK_DIM)
                v_cols = slice(head * ML_V_DIM, (head + 1) * ML_V_DIM)
                scan = int(reverse) * ML_HEADS + head
                h_ref[0, rows, v_cols] = _mlstm_chunk(
                    qt_ref[0, c, qk_cols, :], k_ref[0, rows, qk_cols], vt_ref[0, c, v_cols, :],
                    gr_ref[0, c], uc_ref[0, rows, :], scan,
                    state_ref.at[scan], m_ref.at[scan], reverse)
        return carry

    lax.fori_loop(0, n_chunks, step, 0)


def _mlstm_scan(qt, k, vt, gr, uc, batch, seq):
    tm = TM_ML
    cpt = tm // ML_CHUNK
    nt = seq // tm
    n_chunks = seq // ML_CHUNK
    qt4 = qt.reshape(batch, n_chunks, ML_QK_COLS, ML_CHUNK)
    k3 = k.reshape(batch, seq, ML_QK_COLS)
    vt4 = vt.reshape(batch, n_chunks, D_MODEL, ML_CHUNK)
    gr4 = gr.reshape(batch, n_chunks, ML_GATE_COLS, ML_CHUNK)
    uc3 = uc.reshape(batch, seq, LANES)

    def specs(pos):
        return [pl.BlockSpec((1, cpt, ML_QK_COLS, ML_CHUNK), lambda b, i: (b, pos(i), 0, 0)),
                pl.BlockSpec((1, tm, ML_QK_COLS), lambda b, i: (b, pos(i), 0)),
                pl.BlockSpec((1, cpt, D_MODEL, ML_CHUNK), lambda b, i: (b, pos(i), 0, 0)),
                pl.BlockSpec((1, cpt, ML_GATE_COLS, ML_CHUNK), lambda b, i: (b, pos(i), 0, 0)),
                pl.BlockSpec((1, tm, LANES), lambda b, i: (b, pos(i), 0))]

    fwd = lambda i: i
    bwd = lambda i: nt - 1 - i
    h_shape = jax.ShapeDtypeStruct((batch, seq, D_MODEL), F32)
    return pl.pallas_call(
        _mlstm_scan_kernel,
        grid=(batch, nt),
        in_specs=specs(fwd) + specs(bwd),
        out_specs=[pl.BlockSpec((1, tm, D_MODEL), lambda b, i: (b, fwd(i), 0)),
                   pl.BlockSpec((1, tm, D_MODEL), lambda b, i: (b, bwd(i), 0))],
        out_shape=[h_shape, h_shape],
        scratch_shapes=[pltpu.VMEM((ML_SCANS, ML_STATE_ROWS, ML_QK_DIM), F32),
                        pltpu.VMEM((ML_SCANS, 8, LANES), F32)],
        compiler_params=_params("parallel", "arbitrary"),
        name="mlstm_scan",
    )(qt4, k3, vt4, gr4, uc3, qt4, k3, vt4, gr4, uc3)


def _row_blocks(tm):
    return [slice(r, r + FFN_SUB_ROWS) for r in range(0, tm, FFN_SUB_ROWS)]


def _ffn_tail(xs, hs, g_ref, wgu_ref, wd_ref, out_ref, blocks):
    g1, g2, g3 = g_ref[0:1], g_ref[1:2], g_ref[2:3]
    x1s = [x + _rms(h, g1) for x, h in zip(xs, hs)]
    xns = [_rms(x1, g2).astype(BF16) for x1 in x1s]
    ys = [None] * len(blocks)
    for lo, hi in FFN_SPLITS:
        acts = []
        for xn in xns:
            gate = _dot(xn, wgu_ref[:, lo:hi])
            up = _dot(xn, wgu_ref[:, FFN_HIDDEN + lo:FFN_HIDDEN + hi])
            acts.append((gate * jax.nn.sigmoid(gate) * up).astype(BF16))
        for s, act in enumerate(acts):
            part = _dot(act, wd_ref[lo:hi, :])
            ys[s] = part if ys[s] is None else ys[s] + part
    for rows, x1, y in zip(blocks, x1s, ys):
        out_ref[rows, :] = x1 + _rms(y, g3)


def _attn_post_kernel(x_ref, a_ref, wo_ref, g_ref, wgu_ref, wd_ref, out_ref):
    blocks = _row_blocks(x_ref.shape[0])
    hs = [_dot(a_ref[rows, :], wo_ref[...]) for rows in blocks]
    xs = [x_ref[rows, :] for rows in blocks]
    _ffn_tail(xs, hs, g_ref, wgu_ref, wd_ref, out_ref, blocks)


def _mlstm_post_kernel(x_ref, hf_ref, hb_ref, o_ref, hg_ref, wo_ref, g_ref, wgu_ref, wd_ref,
                       out_ref):
    blocks = _row_blocks(x_ref.shape[0])
    hs = []
    for rows in blocks:
        hsum = hf_ref[rows, :] + hb_ref[rows, :]
        heads = []
        for hd in range(ML_HEADS):
            cols = slice(hd * ML_V_DIM, (hd + 1) * ML_V_DIM)
            heads.append(_rms(hsum[:, cols], hg_ref[:, cols]))
        a = (jnp.concatenate(heads, axis=1) * jax.nn.sigmoid(o_ref[rows, :])).astype(BF16)
        hs.append(_dot(a, wo_ref[...]))
    xs = [x_ref[rows, :] for rows in blocks]
    _ffn_tail(xs, hs, g_ref, wgu_ref, wd_ref, out_ref, blocks)


def _pool_post_kernel(x_ref, prev_ref, next_ref, g0_ref, wp_ref, bp_ref, sp_ref, g_ref,
                      wgu_ref, wd_ref, out_ref, ext_ref, *, tiles_per_seq):
    tm = x_ref.shape[0]
    hl = POOL_HALO
    pos = pl.program_id(0) % tiles_per_seq
    x = x_ref[...]
    g0 = g0_ref[...]
    hn = _rms(x, g0)
    ext_ref[0:hl, :] = jnp.where(pos > 0, _rms(prev_ref[...], g0), 0.0)
    ext_ref[hl:hl + tm, :] = hn
    ext_ref[hl + tm:, :] = jnp.where(pos < tiles_per_seq - 1, _rms(next_ref[...], g0), 0.0)
    seq = tiles_per_seq * tm
    blocks = _row_blocks(tm)
    hs = []
    for rows in blocks:
        r0, nr = rows.start, rows.stop - rows.start
        t = pos * tm + r0 + lax.broadcasted_iota(jnp.int32, (nr, 1), 0)
        parts = []
        for g, w in enumerate(POOL_WINDOWS):
            cols = slice(g * POOL_GROUP_DIM, (g + 1) * POOL_GROUP_DIM)
            base = hl + r0
            tot = ext_ref[base - w // 2:base - w // 2 + nr, cols]
            for off in range(-w // 2 + 1, w // 2):
                tot = tot + ext_ref[base + off:base + off + nr, cols]
            cnt = (jnp.minimum(t + w // 2, seq) - jnp.maximum(t - w // 2, 0)).astype(F32)
            d = (tot / cnt - ext_ref[base:base + nr, cols]).astype(BF16)
            parts.append(_dot(d, wp_ref[g]))
        hs.append((jnp.concatenate(parts, axis=1) + bp_ref[...]) * sp_ref[...])
    xs = [x_ref[rows, :] for rows in blocks]
    _ffn_tail(xs, hs, g_ref, wgu_ref, wd_ref, out_ref, blocks)


def _ffn_specs():
    return [_const_spec((3, D_MODEL)), _const_spec((D_MODEL, 2 * FFN_HIDDEN)),
            _const_spec((FFN_HIDDEN, D_MODEL))]


def _attn_post(x2d, a, wo, gains, wgu, wd):
    t = x2d.shape[0]
    row = pl.BlockSpec((TM_FFN, D_MODEL), lambda i: (i, 0))
    return pl.pallas_call(
        _attn_post_kernel,
        grid=(t // TM_FFN,),
        in_specs=[row, row, _const_spec((D_MODEL, D_MODEL))] + _ffn_specs(),
        out_specs=row,
        out_shape=jax.ShapeDtypeStruct((t, D_MODEL), F32),
        compiler_params=_params("parallel"),
        name="attn_post_ffn",
    )(x2d, a, wo, gains, wgu, wd)


def _mlstm_post(x2d, hf, hb, o, head_gain, wo, gains, wgu, wd):
    t = x2d.shape[0]
    row = pl.BlockSpec((TM_FFN, D_MODEL), lambda i: (i, 0))
    return pl.pallas_call(
        _mlstm_post_kernel,
        grid=(t // TM_FFN,),
        in_specs=[row, row, row, row, _const_spec((1, D_MODEL)),
                  _const_spec((D_MODEL, D_MODEL))] + _ffn_specs(),
        out_specs=row,
        out_shape=jax.ShapeDtypeStruct((t, D_MODEL), F32),
        compiler_params=_params("parallel"),
        name="mlstm_post_ffn",
    )(x2d, hf, hb, o, head_gain, wo, gains, wgu, wd)


def _pool_post(x2d, g0, wp, bp, sp, gains, wgu, wd, seq):
    t = x2d.shape[0]
    tm = TM_FFN
    tps = seq // tm
    hpt = tm // POOL_HALO
    n_halo = t // POOL_HALO
    row = pl.BlockSpec((tm, D_MODEL), lambda i: (i, 0))
    prev = pl.BlockSpec((POOL_HALO, D_MODEL), lambda i: (jnp.maximum(i * hpt - 1, 0), 0))
    nxt = pl.BlockSpec((POOL_HALO, D_MODEL),
                       lambda i: (jnp.minimum((i + 1) * hpt, n_halo - 1), 0))
    kern = functools.partial(_pool_post_kernel, tiles_per_seq=tps)
    return pl.pallas_call(
        kern,
        grid=(t // tm,),
        in_specs=[row, prev, nxt, _const_spec((1, D_MODEL)),
                  _const_spec((POOL_GROUPS, POOL_GROUP_DIM, POOL_GROUP_DIM)),
                  _const_spec((1, D_MODEL)), _const_spec((1, D_MODEL))] + _ffn_specs(),
        out_specs=row,
        out_shape=jax.ShapeDtypeStruct((t, D_MODEL), F32),
        scratch_shapes=[pltpu.VMEM((tm + 2 * POOL_HALO, D_MODEL), F32)],
        compiler_params=_params("parallel"),
        name="pool_post_ffn",
    )(x2d, x2d, x2d, g0, wp, bp, sp, gains, wgu, wd)


def _rope_tables(seq):
    inv = ROPE_THETA ** (-jnp.arange(0, ROPE_DIM, 2, dtype=F32) / ROPE_DIM)
    ang = jnp.arange(seq, dtype=F32)[:, None] * inv[None, :]
    cos, sin = jnp.cos(ang), jnp.sin(ang)
    pad = DA_QK_DIM - ROPE_DIM
    cos_c = jnp.concatenate([cos, cos, jnp.ones((seq, pad), F32)], axis=1)
    sin_c = jnp.concatenate([-sin, sin, jnp.zeros((seq, pad), F32)], axis=1)
    return jnp.tile(cos_c, (1, 2)), jnp.tile(sin_c, (1, 2))


def _lambda_init(layer_idx):
    return 0.8 - 0.6 * math.exp(-0.3 * layer_idx)


def _trunk(x, w):
    batch, seq, _ = x.shape
    x2d = x.reshape(batch * seq, D_MODEL)
    cos_t, sin_t = _rope_tables(seq)
    for i in range(DEPTH):
        kind, slot = i % N_MIXERS, i // N_MIXERS
        g0 = w["norm_gain"][i, 0:1]
        gains = w["norm_gain"][i, 1:4]
        wgu, wd = w["w_ffn_gate_up"][i], w["w_ffn_down"][i]
        if kind == 0:
            q, k, vt = _attn_proj(x2d, g0, w["wq"][slot], w["wk"][slot], w["wvt"][slot],
                                  cos_t, sin_t, seq)
            a = _diff_attn(q, k, vt, w["attn_lambda"][slot], w["subln_col"][slot],
                           batch, seq, _lambda_init(i))
            x2d = _attn_post(x2d, a, w["w_attn_out"][slot], gains, wgu, wd)
        elif kind == 1:
            x2d = _pool_post(x2d, g0, w["w_pool"][slot], w["b_pool"][slot],
                             w["pool_scale"][slot], gains, wgu, wd, seq)
        else:
            qt, k, vt, o, gr, uc = _mlstm_proj(
                x2d, g0, w["ml_wqt"][slot], w["ml_wk"][slot], w["ml_wvt"][slot],
                w["ml_wo"][slot], w["ml_wgt"][slot], w["ml_bgt"][slot])
            hf, hb = _mlstm_scan(qt, k, vt, gr, uc, batch, seq)
            x2d = _mlstm_post(x2d, hf.reshape(-1, D_MODEL), hb.reshape(-1, D_MODEL), o,
                              w["ml_head_gain"][slot], w["w_mlstm_out"][slot], gains, wgu, wd)
    return x2d.reshape(batch, seq, D_MODEL)


def _prepare_weights(norm_gain, w_attn_in, attn_lambda, attn_subln_gain, w_attn_out, w_pool,
                     b_pool, pool_scale, w_mlstm_in, b_mlstm_gate, mlstm_head_gain,
                     w_mlstm_out, w_ffn_gate_up, w_ffn_down):
    d, qk = D_MODEL, ML_QK_COLS
    swap = lambda a: jnp.swapaxes(a, -1, -2)
    wm = w_mlstm_in
    wg = wm[:, :, 2 * qk + 2 * d:]
    nh = ML_HEADS
    gate_order = jnp.array([kind * nh + h for kind in (1, 3, 0, 2) for h in range(nh)])
    return {
        "norm_gain": norm_gain,
        "wq": w_attn_in[:, :, :d].astype(BF16),
        "wk": w_attn_in[:, :, d:2 * d].astype(BF16),
        "wvt": swap(w_attn_in[:, :, 2 * d:]).astype(BF16),
        "attn_lambda": attn_lambda,
        "subln_col": attn_subln_gain[:, :, None],
        "w_attn_out": w_attn_out.astype(BF16),
        "w_pool": w_pool.astype(BF16),
        "b_pool": b_pool[:, None, :],
        "pool_scale": pool_scale[:, None, :],
        "ml_wqt": swap(wm[:, :, :qk]).astype(BF16),
        "ml_wk": wm[:, :, qk:2 * qk].astype(BF16),
        "ml_wvt": swap(wm[:, :, 2 * qk:2 * qk + d]).astype(BF16),
        "ml_wo": wm[:, :, 2 * qk + d:2 * qk + 2 * d].astype(BF16),
        "ml_wgt": swap(wg[:, :, gate_order]).astype(BF16),
        "ml_bgt": b_mlstm_gate[:, gate_order, None],
        "ml_head_gain": mlstm_head_gain[:, None, :],
        "w_mlstm_out": w_mlstm_out.astype(BF16),
        "w_ffn_gate_up": w_ffn_gate_up.astype(BF16),
        "w_ffn_down": w_ffn_down.astype(BF16),
    }


def kernel(x_prompt, x_sample, norm_gain, w_attn_in, attn_lambda, attn_subln_gain, w_attn_out,
           w_pool, b_pool, pool_scale, w_mlstm_in, b_mlstm_gate, mlstm_head_gain, w_mlstm_out,
           w_ffn_gate_up, w_ffn_down):
    w = _prepare_weights(norm_gain, w_attn_in, attn_lambda, attn_subln_gain, w_attn_out,
                         w_pool, b_pool, pool_scale, w_mlstm_in, b_mlstm_gate,
                         mlstm_head_gain, w_mlstm_out, w_ffn_gate_up, w_ffn_down)
    return _trunk(x_prompt, w), _trunk(x_sample, w)
```

```python
import functools
import math

import jax
import jax.numpy as jnp
from jax import lax
from jax.experimental import pallas as pl
from jax.experimental.pallas import tpu as pltpu

D_MODEL = 1024
DEPTH = 4
N_MIXERS = 3
DA_HEADS = 8
DA_QK_DIM = 64
DA_V_DIM = 128
ROPE_DIM = 16
ROPE_THETA = 500000.0
POOL_GROUPS = 4
POOL_WINDOWS = (2, 4, 8, 16)
POOL_GROUP_DIM = D_MODEL // POOL_GROUPS
POOL_HALO = 8
ML_HEADS = 4
ML_QK_DIM = 128
ML_V_DIM = 256
ML_CHUNK = 128
ML_QK_COLS = ML_HEADS * ML_QK_DIM
ML_GATE_COLS = 4 * ML_HEADS
FFN_HIDDEN = 2816
NORM_EPS = 1e-6
LOG2E = math.log2(math.e)

LANES = 128
VMEM_LIMIT = 56 * 1024 * 1024
TM_PROJ = 512
TM_FFN = 512
FFN_SUB_ROWS = 256
FFN_SPLITS = ((0, 1536), (1536, FFN_HIDDEN))
TQ_ATTN = 256
TM_ML = 512

BF16 = jnp.bfloat16
F32 = jnp.float32


def _params(*sem):
    return pltpu.CompilerParams(dimension_semantics=sem, vmem_limit_bytes=VMEM_LIMIT)


def _const_spec(shape):
    nd = len(shape)
    return pl.BlockSpec(shape, lambda *_: (0,) * nd, pipeline_mode=pl.Buffered(1))


def _rms(x, gain):
    ms = jnp.mean(x * x, axis=-1, keepdims=True)
    return x * lax.rsqrt(ms + NORM_EPS) * gain


def _dot(a, b):
    return jnp.dot(a, b, preferred_element_type=F32)


def _dot_nt(a, b):
    return lax.dot_general(a, b, (((1,), (1,)), ((), ())), preferred_element_type=F32)


def _attn_proj_kernel(x_ref, g_ref, wq_ref, wk_ref, wvt_ref, cos_ref, sin_ref,
                      q_ref, k_ref, vt_ref):
    xn = _rms(x_ref[...], g_ref[...]).astype(BF16)
    cos = cos_ref[...]
    sin = sin_ref[...]
    lane = lax.broadcasted_iota(jnp.int32, (1, LANES), 1) % DA_QK_DIM
    first_half = lane < ROPE_DIM // 2

    def rope(y, scale):
        outs = []
        for h in range(DA_HEADS):
            slab = y[:, h * LANES:(h + 1) * LANES]
            partner = jnp.where(first_half,
                                pltpu.roll(slab, LANES - ROPE_DIM // 2, 1),
                                pltpu.roll(slab, ROPE_DIM // 2, 1))
            outs.append(((slab * cos + partner * sin) * scale).astype(BF16))
        return jnp.concatenate(outs, axis=1)

    q_ref[...] = rope(_dot(xn, wq_ref[...]), DA_QK_DIM ** -0.5 * LOG2E)
    k_ref[...] = rope(_dot(xn, wk_ref[...]), 1.0)
    vt_ref[0] = _dot_nt(wvt_ref[...], xn).astype(BF16)


def _attn_proj(x2d, gain, wq, wk, wvt, cos_t, sin_t, seq):
    t = x2d.shape[0]
    tm = TM_PROJ
    nps = seq // tm
    row = pl.BlockSpec((tm, D_MODEL), lambda i: (i, 0))
    tab = pl.BlockSpec((tm, LANES), lambda i: (i % nps, 0))
    return pl.pallas_call(
        _attn_proj_kernel,
        grid=(t // tm,),
        in_specs=[row, _const_spec((1, D_MODEL)), _const_spec((D_MODEL, D_MODEL)),
                  _const_spec((D_MODEL, D_MODEL)), _const_spec((D_MODEL, D_MODEL)), tab, tab],
        out_specs=[row, row, pl.BlockSpec((1, D_MODEL, tm), lambda i: (i, 0, 0))],
        out_shape=[jax.ShapeDtypeStruct((t, D_MODEL), BF16),
                   jax.ShapeDtypeStruct((t, D_MODEL), BF16),
                   jax.ShapeDtypeStruct((t // tm, D_MODEL, tm), BF16)],
        compiler_params=_params("parallel"),
        name="attn_proj",
    )(x2d, gain, wq, wk, wvt, cos_t, sin_t)


DV_AUG = DA_V_DIM + 16


def _diff_attn_kernel(q_ref, k_ref, vt_ref, lam_ref, gain_ref, o_ref, acc_ref, fin_ref,
                      sa_ref, sb_ref, *, n_kv, tk, tq, unroll, q_unroll, lambda_init):
    n_q = q_ref.shape[1] // tq
    n_pairs = n_kv // 2
    lane = lax.broadcasted_iota(jnp.int32, (1, LANES), 1)
    lp = lam_ref[...]
    lam = (jnp.exp(jnp.sum(lp[0:1] * lp[1:2], keepdims=True))
           - jnp.exp(jnp.sum(lp[2:3] * lp[3:4], keepdims=True)) + lambda_init)
    out_gain = gain_ref[...] * (1.0 - lambda_init)
    ones_rows = jnp.ones((DV_AUG - DA_V_DIM, tk), BF16)
    neg = jnp.full((1, tq), -jnp.inf, F32)

    def masked_q(qi):
        q = q_ref[0, pl.ds(pl.multiple_of(qi * tq, tq), tq), :]
        zero = jnp.zeros_like(q)
        return jnp.where(lane < DA_QK_DIM, q, zero), jnp.where(lane >= DA_QK_DIM, q, zero)

    def scores(qm, j, dst_ref):
        kc = k_ref[0, pl.ds(pl.multiple_of(j * tk, tk), tk), :]
        for c in range(2):
            dst_ref[c] = _dot_nt(kc, qm[c])

    def consume(j, src_ref, m):
        vt = jnp.concatenate([vt_ref[0, j], ones_rows], axis=0)
        m_out = []
        for c in range(2):
            s = src_ref[c]
            m_new = jnp.maximum(m[c], jnp.max(s, axis=0, keepdims=True))
            alpha = jnp.exp2(m[c] - m_new)
            p = jnp.exp2(s - m_new).astype(BF16)
            acc_ref[c] = alpha * acc_ref[c] + _dot(vt, p)
            m_out.append(m_new)
        return tuple(m_out)

    def finalize(qi):
        a1, a2 = fin_ref[0], fin_ref[1]
        o = (a1[:DA_V_DIM] / a1[DA_V_DIM:DA_V_DIM + 1]
             - lam * (a2[:DA_V_DIM] / a2[DA_V_DIM:DA_V_DIM + 1]))
        ms = jnp.mean(o * o, axis=0, keepdims=True)
        y = o * lax.rsqrt(ms + NORM_EPS) * out_gain
        start = qi * tq if isinstance(qi, int) else pl.multiple_of(qi * tq, tq)
        o_ref[0, pl.ds(start, tq), :] = y.T.astype(BF16)

    def q_tile(qi, carry):
        qm = masked_q(qi)
        qm_next = masked_q(jnp.minimum(qi + 1, n_q - 1))
        finalize(jnp.maximum(qi - 1, 0))

        def pair(i, m):
            j = 2 * i
            scores(qm, j + 1, sb_ref)
            m = consume(j, sa_ref, m)
            scores(qm, j + 2, sa_ref)
            return consume(j + 1, sb_ref, m)

        m = lax.fori_loop(0, n_pairs - 1, pair, (neg, neg), unroll=unroll)
        scores(qm, n_kv - 1, sb_ref)
        m = consume(n_kv - 2, sa_ref, m)
        scores(qm_next, 0, sa_ref)
        consume(n_kv - 1, sb_ref, m)
        fin_ref[...] = acc_ref[...]
        return carry

    acc_ref[...] = jnp.zeros_like(acc_ref)
    fin_ref[...] = jnp.ones_like(fin_ref)
    scores(masked_q(0), 0, sa_ref)
    lax.fori_loop(0, n_q, q_tile, 0, unroll=q_unroll)
    finalize(n_q - 1)


def _diff_attn(q, k, vt, lam_params, gain_col, batch, seq, lambda_init):
    tq, tk = TQ_ATTN, TM_PROJ
    n_kv = seq // tk
    assert n_kv % 2 == 0 and seq % tq == 0
    n_loop = n_kv // 2 - 1
    unroll = True if n_loop <= 1 else (5 if n_loop % 5 == 0 else 1)
    q3 = q.reshape(batch, seq, D_MODEL)
    k3 = k.reshape(batch, seq, D_MODEL)
    vt4 = vt.reshape(batch, n_kv, D_MODEL, tk)
    q_unroll = 4 if n_loop <= 1 and (seq // tq) % 4 == 0 else 1
    kern = functools.partial(_diff_attn_kernel, n_kv=n_kv, tk=tk, tq=tq, unroll=unroll,
                             q_unroll=q_unroll, lambda_init=lambda_init)
    head = pl.BlockSpec((1, seq, LANES), lambda b, h: (b, 0, h))
    out = pl.pallas_call(
        kern,
        grid=(batch, DA_HEADS),
        in_specs=[head, head,
                  pl.BlockSpec((1, n_kv, DA_V_DIM, tk), lambda b, h: (b, 0, h, 0)),
                  pl.BlockSpec((4, DA_QK_DIM), lambda b, h: (0, 0)),
                  pl.BlockSpec((DA_V_DIM, 1), lambda b, h: (0, 0))],
        out_specs=head,
        out_shape=jax.ShapeDtypeStruct((batch, seq, D_MODEL), BF16),
        scratch_shapes=[pltpu.VMEM((2, DV_AUG, tq), F32), pltpu.VMEM((2, DV_AUG, tq), F32),
                        pltpu.VMEM((2, tk, tq), F32), pltpu.VMEM((2, tk, tq), F32)],
        compiler_params=_params("parallel", "parallel"),
        name="diff_attn",
    )(q3, k3, vt4, lam_params, gain_col)
    return out.reshape(batch * seq, D_MODEL)


ML_SCANS = 2 * ML_HEADS


def _mlstm_proj_kernel(x_ref, g_ref, wqt_ref, wk_ref, wvt_ref, wo_ref, wgt_ref, bgt_ref,
                       qt_ref, k_ref, vt_ref, o_ref, gr_ref, uc_ref):
    tm = x_ref.shape[0]
    xn = _rms(x_ref[...], g_ref[...]).astype(BF16)

    gates = _dot_nt(wgt_ref[...], xn) + bgt_ref[...]
    fg = gates[:ML_SCANS]
    log_f = jnp.minimum(fg, 0.0) - jnp.log1p(jnp.exp(-jnp.abs(fg)))
    pos = lax.broadcasted_iota(jnp.int32, (1, tm), 1) % ML_CHUNK
    pre = suf = log_f
    shift = 1
    while shift < ML_CHUNK:
        pre = pre + jnp.where(pos >= shift, pltpu.roll(pre, shift, 1), 0.0)
        suf = suf + jnp.where(pos < ML_CHUNK - shift, pltpu.roll(suf, tm - shift, 1), 0.0)
        shift *= 2
    is_fwd = lax.broadcasted_iota(jnp.int32, (ML_SCANS, 1), 0) < ML_HEADS
    b = jnp.where(is_fwd, pre, suf)
    u = gates[ML_SCANS:] - b
    gr = jnp.concatenate([b, u], axis=0)
    uc_ref[...] = jnp.concatenate([u, jnp.zeros((LANES - ML_SCANS, tm), F32)], axis=0).T

    qt = _dot_nt(wqt_ref[...], xn).astype(BF16)
    k_ref[...] = (_dot(xn, wk_ref[...]) * ML_QK_DIM ** -0.5).astype(BF16)
    vt = _dot_nt(wvt_ref[...], xn).astype(BF16)
    o_ref[...] = _dot(xn, wo_ref[...])
    for c in range(tm // ML_CHUNK):
        cols = slice(c * ML_CHUNK, (c + 1) * ML_CHUNK)
        qt_ref[0, c] = qt[:, cols]
        vt_ref[0, c] = vt[:, cols]
        gr_ref[0, c] = gr[:, cols]


def _mlstm_proj(x2d, gain, wqt, wk, wvt, wo, wgt, bgt):
    t = x2d.shape[0]
    tm = TM_PROJ
    cpt = tm // ML_CHUNK
    row = lambda n: pl.BlockSpec((tm, n), lambda i: (i, 0))
    chunks = lambda n: pl.BlockSpec((1, cpt, n, ML_CHUNK), lambda i: (i, 0, 0, 0))
    chunk_shape = lambda n, dt: jax.ShapeDtypeStruct((t // tm, cpt, n, ML_CHUNK), dt)
    return pl.pallas_call(
        _mlstm_proj_kernel,
        grid=(t // tm,),
        in_specs=[row(D_MODEL), _const_spec((1, D_MODEL)),
                  _const_spec((ML_QK_COLS, D_MODEL)), _const_spec((D_MODEL, ML_QK_COLS)),
                  _const_spec((D_MODEL, D_MODEL)), _const_spec((D_MODEL, D_MODEL)),
                  _const_spec((ML_GATE_COLS, D_MODEL)), _const_spec((ML_GATE_COLS, 1))],
        out_specs=[chunks(ML_QK_COLS), row(ML_QK_COLS), chunks(D_MODEL), row(D_MODEL),
                   chunks(ML_GATE_COLS), row(LANES)],
        out_shape=[chunk_shape(ML_QK_COLS, BF16),
                   jax.ShapeDtypeStruct((t, ML_QK_COLS), BF16),
                   chunk_shape(D_MODEL, BF16),
                   jax.ShapeDtypeStruct((t, D_MODEL), F32),
                   chunk_shape(ML_GATE_COLS, F32),
                   jax.ShapeDtypeStruct((t, LANES), F32)],
        compiler_params=_params("parallel"),
        name="mlstm_proj",
    )(x2d, gain, wqt, wk, wvt, wo, wgt, bgt)


ML_STATE_ROWS = ML_V_DIM + 16


def _mlstm_chunk_state(qt, k, vt, gr, uc, scan, state_ref, m_ref, reverse):
    L = ML_CHUNK
    b_row = gr[scan:scan + 1]
    u_row = gr[ML_SCANS + scan:ML_SCANS + scan + 1]
    u_col = uc[:, scan:scan + 1]
    b_last = b_row[:, 0:1] if reverse else b_row[:, L - 1:L]
    s_idx = lax.broadcasted_iota(jnp.int32, (L, L), 0)
    j_idx = lax.broadcasted_iota(jnp.int32, (L, L), 1)
    keep = (s_idx >= j_idx) if reverse else (s_idx <= j_idx)
    m_old = m_ref[0:1, 0:1]

    dlog = jnp.where(keep, u_col + b_row, -jnp.inf)
    g = b_row + m_old
    m_row = jnp.maximum(g, jnp.max(dlog, axis=0, keepdims=True))
    w_intra = jnp.exp(dlog - m_row)
    w_inter = jnp.exp(g - m_row)

    st_raw = _dot(k, qt)
    state = state_ref[...]
    inter = _dot(state.astype(BF16), qt)

    w_state_log = b_last + u_row
    m_new = jnp.maximum(b_last + m_old, jnp.max(w_state_log, axis=1, keepdims=True))
    w_state = jnp.exp(w_state_log - m_new)
    decay = jnp.exp(b_last + m_old - m_new)
    vt_aug = jnp.concatenate(
        [vt.astype(F32), jnp.ones((ML_STATE_ROWS - ML_V_DIM, L), F32)], axis=0)
    state_ref[...] = decay * state + _dot((vt_aug * w_state).astype(BF16), k)
    m_ref[...] = jnp.broadcast_to(m_new, m_ref.shape)
    return st_raw, inter, w_intra, w_inter, m_row


def _mlstm_chunk_output(vt, st_raw, inter, w_intra, w_inter, m_row):
    st = st_raw * w_intra
    num = w_inter * inter[:ML_V_DIM] + _dot(vt, st.astype(BF16))
    den = w_inter * inter[ML_V_DIM:ML_V_DIM + 1] + jnp.sum(st, axis=0, keepdims=True)
    ht = num / jnp.maximum(jnp.abs(den), jnp.exp(-m_row))
    return ht.T


def _mlstm_scan_kernel(qtf_ref, kf_ref, vtf_ref, grf_ref, ucf_ref,
                       qtb_ref, kb_ref, vtb_ref, grb_ref, ucb_ref,
                       hf_ref, hb_ref, state_ref, m_ref):
    @pl.when(pl.program_id(1) == 0)
    def _():
        state_ref[...] = jnp.zeros_like(state_ref)
        m_ref[...] = jnp.zeros_like(m_ref)

    n_chunks = kf_ref.shape[1] // ML_CHUNK
    fwd_refs = (qtf_ref, kf_ref, vtf_ref, grf_ref, ucf_ref, hf_ref)
    bwd_refs = (qtb_ref, kb_ref, vtb_ref, grb_ref, ucb_ref, hb_ref)

    def step(i, carry):
        pending = []
        for head in range(ML_HEADS):
            for reverse in (False, True):
                qt_ref, k_ref, vt_ref, gr_ref, uc_ref, h_ref = bwd_refs if reverse else fwd_refs
                c = n_chunks - 1 - i if reverse else i
                rows = pl.ds(pl.multiple_of(c * ML_CHUNK, ML_CHUNK), ML_CHUNK)
                qk_cols = slice(head * ML_QK_DIM, (head + 1) * ML_QK_DIM)
                v_cols = slice(head * ML_V_DIM, (head + 1) * ML_V_DIM)
                scan = int(reverse) * ML_HEADS + head
                vt = vt_ref[0, c, v_cols, :]
                parts = _mlstm_chunk_state(
                    qt_ref[0, c, qk_cols, :], k_ref[0, rows, qk_cols], vt,
                    gr_ref[0, c], uc_ref[0, rows, :], scan,
                    state_ref.at[scan], m_ref.at[scan], reverse)
                pending.append((h_ref, rows, v_cols, vt, parts))
        for h_ref, rows, v_cols, vt, parts in pending:
            h_ref[0, rows, v_cols] = _mlstm_chunk_output(vt, *parts)
        return carry

    lax.fori_loop(0, n_chunks, step, 0)


def _mlstm_scan(qt, k, vt, gr, uc, batch, seq):
    tm = TM_ML
    cpt = tm // ML_CHUNK
    nt = seq // tm
    n_chunks = seq // ML_CHUNK
    qt4 = qt.reshape(batch, n_chunks, ML_QK_COLS, ML_CHUNK)
    k3 = k.reshape(batch, seq, ML_QK_COLS)
    vt4 = vt.reshape(batch, n_chunks, D_MODEL, ML_CHUNK)
    gr4 = gr.reshape(batch, n_chunks, ML_GATE_COLS, ML_CHUNK)
    uc3 = uc.reshape(batch, seq, LANES)

    def specs(pos):
        return [pl.BlockSpec((1, cpt, ML_QK_COLS, ML_CHUNK), lambda b, i: (b, pos(i), 0, 0)),
                pl.BlockSpec((1, tm, ML_QK_COLS), lambda b, i: (b, pos(i), 0)),
                pl.BlockSpec((1, cpt, D_MODEL, ML_CHUNK), lambda b, i: (b, pos(i), 0, 0)),
                pl.BlockSpec((1, cpt, ML_GATE_COLS, ML_CHUNK), lambda b, i: (b, pos(i), 0, 0)),
                pl.BlockSpec((1, tm, LANES), lambda b, i: (b, pos(i), 0))]

    fwd = lambda i: i
    bwd = lambda i: nt - 1 - i
    h_shape = jax.ShapeDtypeStruct((batch, seq, D_MODEL), F32)
    return pl.pallas_call(
        _mlstm_scan_kernel,
        grid=(batch, nt),
        in_specs=specs(fwd) + specs(bwd),
        out_specs=[pl.BlockSpec((1, tm, D_MODEL), lambda b, i: (b, fwd(i), 0)),
                   pl.BlockSpec((1, tm, D_MODEL), lambda b, i: (b, bwd(i), 0))],
        out_shape=[h_shape, h_shape],
        scratch_shapes=[pltpu.VMEM((ML_SCANS, ML_STATE_ROWS, ML_QK_DIM), F32),
                        pltpu.VMEM((ML_SCANS, 8, LANES), F32)],
        compiler_params=_params("parallel", "arbitrary"),
        name="mlstm_scan",
    )(qt4, k3, vt4, gr4, uc3, qt4, k3, vt4, gr4, uc3)


def _row_blocks(tm):
    return [slice(r, r + FFN_SUB_ROWS) for r in range(0, tm, FFN_SUB_ROWS)]


def _ffn_tail(x_ref, mixer_out, g_ref, wgu_ref, wd_ref, out_ref):
    g1, g2, g3 = g_ref[0:1], g_ref[1:2], g_ref[2:3]
    blocks = _row_blocks(x_ref.shape[0])
    hs = [mixer_out(rows) for rows in blocks]
    x1s = [x_ref[rows, :] + _rms(h, g1) for rows, h in zip(blocks, hs)]
    xns = [_rms(x1, g2).astype(BF16) for x1 in x1s]
    ys = [None] * len(blocks)
    for lo, hi in FFN_SPLITS:
        acts = []
        for xn in xns:
            gate = _dot(xn, wgu_ref[:, lo:hi])
            up = _dot(xn, wgu_ref[:, FFN_HIDDEN + lo:FFN_HIDDEN + hi])
            acts.append((gate * jax.nn.sigmoid(gate) * up).astype(BF16))
        for s, act in enumerate(acts):
            part = _dot(act, wd_ref[lo:hi, :])
            ys[s] = part if ys[s] is None else ys[s] + part
    for rows, x1, y in zip(blocks, x1s, ys):
        out_ref[rows, :] = x1 + _rms(y, g3)


def _attn_post_kernel(x_ref, a_ref, wo_ref, g_ref, wgu_ref, wd_ref, out_ref):
    def mixer_out(rows):
        return _dot(a_ref[rows, :], wo_ref[...])

    _ffn_tail(x_ref, mixer_out, g_ref, wgu_ref, wd_ref, out_ref)


def _mlstm_post_kernel(x_ref, hf_ref, hb_ref, o_ref, hg_ref, wo_ref, g_ref, wgu_ref, wd_ref,
                       out_ref):
    def mixer_out(rows):
        hsum = hf_ref[rows, :] + hb_ref[rows, :]
        heads = []
        for hd in range(ML_HEADS):
            cols = slice(hd * ML_V_DIM, (hd + 1) * ML_V_DIM)
            heads.append(_rms(hsum[:, cols], hg_ref[:, cols]))
        a = (jnp.concatenate(heads, axis=1) * jax.nn.sigmoid(o_ref[rows, :])).astype(BF16)
        return _dot(a, wo_ref[...])

    _ffn_tail(x_ref, mixer_out, g_ref, wgu_ref, wd_ref, out_ref)


def _pool_post_kernel(x_ref, prev_ref, next_ref, g0_ref, wp_ref, bp_ref, sp_ref, g_ref,
                      wgu_ref, wd_ref, out_ref, ext_ref, *, tiles_per_seq):
    tm = x_ref.shape[0]
    hl = POOL_HALO
    pos = pl.program_id(0) % tiles_per_seq
    x = x_ref[...]
    g0 = g0_ref[...]
    hn = _rms(x, g0)
    ext_ref[0:hl, :] = jnp.where(pos > 0, _rms(prev_ref[...], g0), 0.0)
    ext_ref[hl:hl + tm, :] = hn
    ext_ref[hl + tm:, :] = jnp.where(pos < tiles_per_seq - 1, _rms(next_ref[...], g0), 0.0)
    seq = tiles_per_seq * tm

    def mixer_out(rows):
        r0, nr = rows.start, rows.stop - rows.start
        t = pos * tm + r0 + lax.broadcasted_iota(jnp.int32, (nr, 1), 0)
        parts = []
        for g, w in enumerate(POOL_WINDOWS):
            cols = slice(g * POOL_GROUP_DIM, (g + 1) * POOL_GROUP_DIM)
            base = hl + r0
            tot = ext_ref[base - w // 2:base - w // 2 + nr, cols]
            for off in range(-w // 2 + 1, w // 2):
                tot = tot + ext_ref[base + off:base + off + nr, cols]
            cnt = (jnp.minimum(t + w // 2, seq) - jnp.maximum(t - w // 2, 0)).astype(F32)
            d = (tot / cnt - ext_ref[base:base + nr, cols]).astype(BF16)
            parts.append(_dot(d, wp_ref[g]))
        return (jnp.concatenate(parts, axis=1) + bp_ref[...]) * sp_ref[...]

    _ffn_tail(x_ref, mixer_out, g_ref, wgu_ref, wd_ref, out_ref)


def _ffn_specs():
    return [_const_spec((3, D_MODEL)), _const_spec((D_MODEL, 2 * FFN_HIDDEN)),
            _const_spec((FFN_HIDDEN, D_MODEL))]


def _attn_post(x2d, a, wo, gains, wgu, wd):
    t = x2d.shape[0]
    row = pl.BlockSpec((TM_FFN, D_MODEL), lambda i: (i, 0))
    return pl.pallas_call(
        _attn_post_kernel,
        grid=(t // TM_FFN,),
        in_specs=[row, row, _const_spec((D_MODEL, D_MODEL))] + _ffn_specs(),
        out_specs=row,
        out_shape=jax.ShapeDtypeStruct((t, D_MODEL), F32),
        compiler_params=_params("parallel"),
        name="attn_post_ffn",
    )(x2d, a, wo, gains, wgu, wd)


def _mlstm_post(x2d, hf, hb, o, head_gain, wo, gains, wgu, wd):
    t = x2d.shape[0]
    row = pl.BlockSpec((TM_FFN, D_MODEL), lambda i: (i, 0))
    return pl.pallas_call(
        _mlstm_post_kernel,
        grid=(t // TM_FFN,),
        in_specs=[row, row, row, row, _const_spec((1, D_MODEL)),
                  _const_spec((D_MODEL, D_MODEL))] + _ffn_specs(),
        out_specs=row,
        out_shape=jax.ShapeDtypeStruct((t, D_MODEL), F32),
        compiler_params=_params("parallel"),
        name="mlstm_post_ffn",
    )(x2d, hf, hb, o, head_gain, wo, gains, wgu, wd)


def _pool_post(x2d, g0, wp, bp, sp, gains, wgu, wd, seq):
    t = x2d.shape[0]
    tm = TM_FFN
    tps = seq // tm
    hpt = tm // POOL_HALO
    n_halo = t // POOL_HALO
    row = pl.BlockSpec((tm, D_MODEL), lambda i: (i, 0))
    prev = pl.BlockSpec((POOL_HALO, D_MODEL), lambda i: (jnp.maximum(i * hpt - 1, 0), 0))
    nxt = pl.BlockSpec((POOL_HALO, D_MODEL),
                       lambda i: (jnp.minimum((i + 1) * hpt, n_halo - 1), 0))
    kern = functools.partial(_pool_post_kernel, tiles_per_seq=tps)
    return pl.pallas_call(
        kern,
        grid=(t // tm,),
        in_specs=[row, prev, nxt, _const_spec((1, D_MODEL)),
                  _const_spec((POOL_GROUPS, POOL_GROUP_DIM, POOL_GROUP_DIM)),
                  _const_spec((1, D_MODEL)), _const_spec((1, D_MODEL))] + _ffn_specs(),
        out_specs=row,
        out_shape=jax.ShapeDtypeStruct((t, D_MODEL), F32),
        scratch_shapes=[pltpu.VMEM((tm + 2 * POOL_HALO, D_MODEL), F32)],
        compiler_params=_params("parallel"),
        name="pool_post_ffn",
    )(x2d, x2d, x2d, g0, wp, bp, sp, gains, wgu, wd)


def _rope_tables(seq):
    inv = ROPE_THETA ** (-jnp.arange(0, ROPE_DIM, 2, dtype=F32) / ROPE_DIM)
    ang = jnp.arange(seq, dtype=F32)[:, None] * inv[None, :]
    cos, sin = jnp.cos(ang), jnp.sin(ang)
    pad = DA_QK_DIM - ROPE_DIM
    cos_c = jnp.concatenate([cos, cos, jnp.ones((seq, pad), F32)], axis=1)
    sin_c = jnp.concatenate([-sin, sin, jnp.zeros((seq, pad), F32)], axis=1)
    return jnp.tile(cos_c, (1, 2)), jnp.tile(sin_c, (1, 2))


def _lambda_init(layer_idx):
    return 0.8 - 0.6 * math.exp(-0.3 * layer_idx)


def _trunk(x, w):
    batch, seq, _ = x.shape
    x2d = x.reshape(batch * seq, D_MODEL)
    cos_t, sin_t = _rope_tables(seq)
    for i in range(DEPTH):
        kind, slot = i % N_MIXERS, i // N_MIXERS
        g0 = w["norm_gain"][i, 0:1]
        gains = w["norm_gain"][i, 1:4]
        wgu, wd = w["w_ffn_gate_up"][i], w["w_ffn_down"][i]
        if kind == 0:
            q, k, vt = _attn_proj(x2d, g0, w["wq"][slot], w["wk"][slot], w["wvt"][slot],
                                  cos_t, sin_t, seq)
            a = _diff_attn(q, k, vt, w["attn_lambda"][slot], w["subln_col"][slot],
                           batch, seq, _lambda_init(i))
            x2d = _attn_post(x2d, a, w["w_attn_out"][slot], gains, wgu, wd)
        elif kind == 1:
            x2d = _pool_post(x2d, g0, w["w_pool"][slot], w["b_pool"][slot],
                             w["pool_scale"][slot], gains, wgu, wd, seq)
        else:
            qt, k, vt, o, gr, uc = _mlstm_proj(
                x2d, g0, w["ml_wqt"][slot], w["ml_wk"][slot], w["ml_wvt"][slot],
                w["ml_wo"][slot], w["ml_wgt"][slot], w["ml_bgt"][slot])
            hf, hb = _mlstm_scan(qt, k, vt, gr, uc, batch, seq)
            x2d = _mlstm_post(x2d, hf.reshape(-1, D_MODEL), hb.reshape(-1, D_MODEL), o,
                              w["ml_head_gain"][slot], w["w_mlstm_out"][slot], gains, wgu, wd)
    return x2d.reshape(batch, seq, D_MODEL)


def _prepare_weights(norm_gain, w_attn_in, attn_lambda, attn_subln_gain, w_attn_out, w_pool,
                     b_pool, pool_scale, w_mlstm_in, b_mlstm_gate, mlstm_head_gain,
                     w_mlstm_out, w_ffn_gate_up, w_ffn_down):
    d, qk = D_MODEL, ML_QK_COLS
    swap = lambda a: jnp.swapaxes(a, -1, -2)
    wm = w_mlstm_in
    wg = wm[:, :, 2 * qk + 2 * d:]
    nh = ML_HEADS
    gate_order = jnp.array([kind * nh + h for kind in (1, 3, 0, 2) for h in range(nh)])
    return {
        "norm_gain": norm_gain,
        "wq": w_attn_in[:, :, :d].astype(BF16),
        "wk": w_attn_in[:, :, d:2 * d].astype(BF16),
        "wvt": swap(w_attn_in[:, :, 2 * d:]).astype(BF16),
        "attn_lambda": attn_lambda,
        "subln_col": attn_subln_gain[:, :, None],
        "w_attn_out": w_attn_out.astype(BF16),
        "w_pool": w_pool.astype(BF16),
        "b_pool": b_pool[:, None, :],
        "pool_scale": pool_scale[:, None, :],
        "ml_wqt": swap(wm[:, :, :qk]).astype(BF16),
        "ml_wk": wm[:, :, qk:2 * qk].astype(BF16),
        "ml_wvt": swap(wm[:, :, 2 * qk:2 * qk + d]).astype(BF16),
        "ml_wo": wm[:, :, 2 * qk + d:2 * qk + 2 * d].astype(BF16),
        "ml_wgt": swap(wg[:, :, gate_order]).astype(BF16),
        "ml_bgt": b_mlstm_gate[:, gate_order, None],
        "ml_head_gain": mlstm_head_gain[:, None, :],
        "w_mlstm_out": w_mlstm_out.astype(BF16),
        "w_ffn_gate_up": w_ffn_gate_up.astype(BF16),
        "w_ffn_down": w_ffn_down.astype(BF16),
    }


def kernel(x_prompt, x_sample, norm_gain, w_attn_in, attn_lambda, attn_subln_gain, w_attn_out,
           w_pool, b_pool, pool_scale, w_mlstm_in, b_mlstm_gate, mlstm_head_gain, w_mlstm_out,
           w_ffn_gate_up, w_ffn_down):
    w = _prepare_weights(norm_gain, w_attn_in, attn_lambda, attn_subln_gain, w_attn_out,
                         w_pool, b_pool, pool_scale, w_mlstm_in, b_mlstm_gate,
                         mlstm_head_gain, w_mlstm_out, w_ffn_gate_up, w_ffn_down)
    return _trunk(x_prompt, w), _trunk(x_sample, w)
```

```python
import functools
import math

import jax
import jax.numpy as jnp
from jax import lax
from jax.experimental import pallas as pl
from jax.experimental.pallas import tpu as pltpu

D_MODEL = 1024
DEPTH = 4
N_MIXERS = 3
DA_HEADS = 8
DA_QK_DIM = 64
DA_V_DIM = 128
ROPE_DIM = 16
ROPE_THETA = 500000.0
POOL_GROUPS = 4
POOL_WINDOWS = (2, 4, 8, 16)
POOL_GROUP_DIM = D_MODEL // POOL_GROUPS
POOL_HALO = 8
ML_HEADS = 4
ML_QK_DIM = 128
ML_V_DIM = 256
ML_CHUNK = 128
ML_QK_COLS = ML_HEADS * ML_QK_DIM
ML_GATE_COLS = 4 * ML_HEADS
FFN_HIDDEN = 2816
NORM_EPS = 1e-6
LOG2E = math.log2(math.e)

LANES = 128
VMEM_LIMIT = 56 * 1024 * 1024
TM_PROJ = 512
TM_FFN = 512
FFN_SUB_ROWS = 256
FFN_SPLITS = ((0, 1536), (1536, FFN_HIDDEN))
TQ_ATTN = 256
TM_ML = 512

BF16 = jnp.bfloat16
F32 = jnp.float32


def _params(*sem):
    return pltpu.CompilerParams(dimension_semantics=sem, vmem_limit_bytes=VMEM_LIMIT)


def _const_spec(shape):
    nd = len(shape)
    return pl.BlockSpec(shape, lambda *_: (0,) * nd, pipeline_mode=pl.Buffered(1))


def _rms(x, gain):
    ms = jnp.mean(x * x, axis=-1, keepdims=True)
    return x * lax.rsqrt(ms + NORM_EPS) * gain


def _dot(a, b):
    return jnp.dot(a, b, preferred_element_type=F32)


def _dot_nt(a, b):
    return lax.dot_general(a, b, (((1,), (1,)), ((), ())), preferred_element_type=F32)


def _attn_proj_kernel(x_ref, g_ref, wq_ref, wk_ref, wvt_ref, cos_ref, sin_ref,
                      q_ref, k_ref, vt_ref):
    xn = _rms(x_ref[...], g_ref[...]).astype(BF16)
    cos = cos_ref[...]
    sin = sin_ref[...]
    lane = lax.broadcasted_iota(jnp.int32, (1, LANES), 1) % DA_QK_DIM
    first_half = lane < ROPE_DIM // 2

    def rope(y, scale):
        outs = []
        for h in range(DA_HEADS):
            slab = y[:, h * LANES:(h + 1) * LANES]
            partner = jnp.where(first_half,
                                pltpu.roll(slab, LANES - ROPE_DIM // 2, 1),
                                pltpu.roll(slab, ROPE_DIM // 2, 1))
            outs.append(((slab * cos + partner * sin) * scale).astype(BF16))
        return jnp.concatenate(outs, axis=1)

    q_ref[...] = rope(_dot(xn, wq_ref[...]), DA_QK_DIM ** -0.5 * LOG2E)
    k_ref[...] = rope(_dot(xn, wk_ref[...]), 1.0)
    vt_ref[0] = _dot_nt(wvt_ref[...], xn).astype(BF16)


def _attn_proj(x2d, gain, wq, wk, wvt, cos_t, sin_t, seq):
    t = x2d.shape[0]
    tm = TM_PROJ
    nps = seq // tm
    row = pl.BlockSpec((tm, D_MODEL), lambda i: (i, 0))
    tab = pl.BlockSpec((tm, LANES), lambda i: (i % nps, 0))
    return pl.pallas_call(
        _attn_proj_kernel,
        grid=(t // tm,),
        in_specs=[row, _const_spec((1, D_MODEL)), _const_spec((D_MODEL, D_MODEL)),
                  _const_spec((D_MODEL, D_MODEL)), _const_spec((D_MODEL, D_MODEL)), tab, tab],
        out_specs=[row, row, pl.BlockSpec((1, D_MODEL, tm), lambda i: (i, 0, 0))],
        out_shape=[jax.ShapeDtypeStruct((t, D_MODEL), BF16),
                   jax.ShapeDtypeStruct((t, D_MODEL), BF16),
                   jax.ShapeDtypeStruct((t // tm, D_MODEL, tm), BF16)],
        compiler_params=_params("parallel"),
        name="attn_proj",
    )(x2d, gain, wq, wk, wvt, cos_t, sin_t)


DV_AUG = DA_V_DIM + 16


def _diff_attn_kernel(q_ref, k_ref, vt_ref, lam_ref, gain_ref, o_ref, acc_ref, fin_ref,
                      sa_ref, sb_ref, *, n_kv, tk, tq, unroll, q_unroll, lambda_init):
    n_q = q_ref.shape[1] // tq
    n_pairs = n_kv // 2
    lane = lax.broadcasted_iota(jnp.int32, (1, LANES), 1)
    lp = lam_ref[...]
    lam = (jnp.exp(jnp.sum(lp[0:1] * lp[1:2], keepdims=True))
           - jnp.exp(jnp.sum(lp[2:3] * lp[3:4], keepdims=True)) + lambda_init)
    out_gain = gain_ref[...] * (1.0 - lambda_init)
    ones_rows = jnp.ones((DV_AUG - DA_V_DIM, tk), BF16)
    neg = jnp.full((1, tq), -jnp.inf, F32)

    def masked_q(qi):
        q = q_ref[0, pl.ds(pl.multiple_of(qi * tq, tq), tq), :]
        zero = jnp.zeros_like(q)
        return jnp.where(lane < DA_QK_DIM, q, zero), jnp.where(lane >= DA_QK_DIM, q, zero)

    def scores(qm, j, dst_ref):
        kc = k_ref[0, pl.ds(pl.multiple_of(j * tk, tk), tk), :]
        for c in range(2):
            dst_ref[c] = _dot_nt(kc, qm[c])

    def consume(j, src_ref, m):
        vt = jnp.concatenate([vt_ref[0, j], ones_rows], axis=0)
        m_out = []
        for c in range(2):
            s = src_ref[c]
            m_new = jnp.maximum(m[c], jnp.max(s, axis=0, keepdims=True))
            alpha = jnp.exp2(m[c] - m_new)
            p = jnp.exp2(s - m_new).astype(BF16)
            acc_ref[c] = alpha * acc_ref[c] + _dot(vt, p)
            m_out.append(m_new)
        return tuple(m_out)

    def finalize(qi):
        a1, a2 = fin_ref[0], fin_ref[1]
        o = (a1[:DA_V_DIM] / a1[DA_V_DIM:DA_V_DIM + 1]
             - lam * (a2[:DA_V_DIM] / a2[DA_V_DIM:DA_V_DIM + 1]))
        ms = jnp.mean(o * o, axis=0, keepdims=True)
        y = o * lax.rsqrt(ms + NORM_EPS) * out_gain
        start = qi * tq if isinstance(qi, int) else pl.multiple_of(qi * tq, tq)
        o_ref[0, pl.ds(start, tq), :] = y.T.astype(BF16)

    def q_tile(qi, carry):
        qm = masked_q(qi)
        qm_next = masked_q(jnp.minimum(qi + 1, n_q - 1))
        finalize(jnp.maximum(qi - 1, 0))

        def pair(i, m):
            j = 2 * i
            scores(qm, j + 1, sb_ref)
            m = consume(j, sa_ref, m)
            scores(qm, j + 2, sa_ref)
            return consume(j + 1, sb_ref, m)

        m = lax.fori_loop(0, n_pairs - 1, pair, (neg, neg), unroll=unroll)
        scores(qm, n_kv - 1, sb_ref)
        m = consume(n_kv - 2, sa_ref, m)
        scores(qm_next, 0, sa_ref)
        consume(n_kv - 1, sb_ref, m)
        fin_ref[...] = acc_ref[...]
        return carry

    acc_ref[...] = jnp.zeros_like(acc_ref)
    fin_ref[...] = jnp.ones_like(fin_ref)
    scores(masked_q(0), 0, sa_ref)
    lax.fori_loop(0, n_q, q_tile, 0, unroll=q_unroll)
    finalize(n_q - 1)


def _diff_attn(q, k, vt, lam_params, gain_col, batch, seq, lambda_init):
    tq, tk = TQ_ATTN, TM_PROJ
    n_kv = seq // tk
    assert n_kv % 2 == 0 and seq % tq == 0
    n_loop = n_kv // 2 - 1
    unroll = True if n_loop <= 1 else (5 if n_loop % 5 == 0 else 1)
    q3 = q.reshape(batch, seq, D_MODEL)
    k3 = k.reshape(batch, seq, D_MODEL)
    vt4 = vt.reshape(batch, n_kv, D_MODEL, tk)
    q_unroll = 4 if n_loop <= 1 and (seq // tq) % 4 == 0 else 1
    kern = functools.partial(_diff_attn_kernel, n_kv=n_kv, tk=tk, tq=tq, unroll=unroll,
                             q_unroll=q_unroll, lambda_init=lambda_init)
    head = pl.BlockSpec((1, seq, LANES), lambda b, h: (b, 0, h))
    out = pl.pallas_call(
        kern,
        grid=(batch, DA_HEADS),
        in_specs=[head, head,
                  pl.BlockSpec((1, n_kv, DA_V_DIM, tk), lambda b, h: (b, 0, h, 0)),
                  pl.BlockSpec((4, DA_QK_DIM), lambda b, h: (0, 0)),
                  pl.BlockSpec((DA_V_DIM, 1), lambda b, h: (0, 0))],
        out_specs=head,
        out_shape=jax.ShapeDtypeStruct((batch, seq, D_MODEL), BF16),
        scratch_shapes=[pltpu.VMEM((2, DV_AUG, tq), F32), pltpu.VMEM((2, DV_AUG, tq), F32),
                        pltpu.VMEM((2, tk, tq), F32), pltpu.VMEM((2, tk, tq), F32)],
        compiler_params=_params("parallel", "parallel"),
        name="diff_attn",
    )(q3, k3, vt4, lam_params, gain_col)
    return out.reshape(batch * seq, D_MODEL)


ML_SCANS = 2 * ML_HEADS


def _mlstm_proj_kernel(x_ref, g_ref, wqt_ref, wk_ref, wvt_ref, wo_ref, wgt_ref, bgt_ref,
                       qt_ref, k_ref, vt_ref, o_ref, gr_ref, uc_ref):
    tm = x_ref.shape[0]
    xn = _rms(x_ref[...], g_ref[...]).astype(BF16)

    gates = _dot_nt(wgt_ref[...], xn) + bgt_ref[...]
    fg = gates[:ML_SCANS]
    log_f = jnp.minimum(fg, 0.0) - jnp.log1p(jnp.exp(-jnp.abs(fg)))
    pos = lax.broadcasted_iota(jnp.int32, (1, tm), 1) % ML_CHUNK
    pre = suf = log_f
    shift = 1
    while shift < ML_CHUNK:
        pre = pre + jnp.where(pos >= shift, pltpu.roll(pre, shift, 1), 0.0)
        suf = suf + jnp.where(pos < ML_CHUNK - shift, pltpu.roll(suf, tm - shift, 1), 0.0)
        shift *= 2
    is_fwd = lax.broadcasted_iota(jnp.int32, (ML_SCANS, 1), 0) < ML_HEADS
    b = jnp.where(is_fwd, pre, suf)
    u = gates[ML_SCANS:] - b
    gr = jnp.concatenate([b, u], axis=0)
    uc_ref[...] = jnp.concatenate([u, jnp.zeros((LANES - ML_SCANS, tm), F32)], axis=0).T

    qt = _dot_nt(wqt_ref[...], xn).astype(BF16)
    k_ref[...] = (_dot(xn, wk_ref[...]) * ML_QK_DIM ** -0.5).astype(BF16)
    vt = _dot_nt(wvt_ref[...], xn).astype(BF16)
    o_ref[...] = _dot(xn, wo_ref[...])
    for c in range(tm // ML_CHUNK):
        cols = slice(c * ML_CHUNK, (c + 1) * ML_CHUNK)
        qt_ref[0, c] = qt[:, cols]
        vt_ref[0, c] = vt[:, cols]
        gr_ref[0, c] = gr[:, cols]


def _mlstm_proj(x2d, gain, wqt, wk, wvt, wo, wgt, bgt):
    t = x2d.shape[0]
    tm = TM_PROJ
    cpt = tm // ML_CHUNK
    row = lambda n: pl.BlockSpec((tm, n), lambda i: (i, 0))
    chunks = lambda n: pl.BlockSpec((1, cpt, n, ML_CHUNK), lambda i: (i, 0, 0, 0))
    chunk_shape = lambda n, dt: jax.ShapeDtypeStruct((t // tm, cpt, n, ML_CHUNK), dt)
    return pl.pallas_call(
        _mlstm_proj_kernel,
        grid=(t // tm,),
        in_specs=[row(D_MODEL), _const_spec((1, D_MODEL)),
                  _const_spec((ML_QK_COLS, D_MODEL)), _const_spec((D_MODEL, ML_QK_COLS)),
                  _const_spec((D_MODEL, D_MODEL)), _const_spec((D_MODEL, D_MODEL)),
                  _const_spec((ML_GATE_COLS, D_MODEL)), _const_spec((ML_GATE_COLS, 1))],
        out_specs=[chunks(ML_QK_COLS), row(ML_QK_COLS), chunks(D_MODEL), row(D_MODEL),
                   chunks(ML_GATE_COLS), row(LANES)],
        out_shape=[chunk_shape(ML_QK_COLS, BF16),
                   jax.ShapeDtypeStruct((t, ML_QK_COLS), BF16),
                   chunk_shape(D_MODEL, BF16),
                   jax.ShapeDtypeStruct((t, D_MODEL), F32),
                   chunk_shape(ML_GATE_COLS, F32),
                   jax.ShapeDtypeStruct((t, LANES), F32)],
        compiler_params=_params("parallel"),
        name="mlstm_proj",
    )(x2d, gain, wqt, wk, wvt, wo, wgt, bgt)


ML_STATE_ROWS = ML_V_DIM + 16


def _mlstm_chunk_state(qt, k, vt, gr, uc, scan, state_ref, m_ref, reverse):
    L = ML_CHUNK
    b_row = gr[scan:scan + 1]
    u_row = gr[ML_SCANS + scan:ML_SCANS + scan + 1]
    u_col = uc[:, scan:scan + 1]
    b_last = b_row[:, 0:1] if reverse else b_row[:, L - 1:L]
    s_idx = lax.broadcasted_iota(jnp.int32, (L, L), 0)
    j_idx = lax.broadcasted_iota(jnp.int32, (L, L), 1)
    keep = (s_idx >= j_idx) if reverse else (s_idx <= j_idx)
    m_old = m_ref[0:1, 0:1]

    dlog = jnp.where(keep, u_col + b_row, -jnp.inf)
    g = b_row + m_old
    m_row = jnp.maximum(g, jnp.max(dlog, axis=0, keepdims=True))
    w_intra = jnp.exp(dlog - m_row)
    w_inter = jnp.exp(g - m_row)

    st_raw = _dot(k, qt)
    state = state_ref[...]
    inter = _dot(state.astype(BF16), qt)

    w_state_log = b_last + u_row
    m_new = jnp.maximum(b_last + m_old, jnp.max(w_state_log, axis=1, keepdims=True))
    w_state = jnp.exp(w_state_log - m_new)
    decay = jnp.exp(b_last + m_old - m_new)
    vt_aug = jnp.concatenate(
        [vt.astype(F32), jnp.ones((ML_STATE_ROWS - ML_V_DIM, L), F32)], axis=0)
    state_ref[...] = decay * state + _dot((vt_aug * w_state).astype(BF16), k)
    m_ref[...] = jnp.broadcast_to(m_new, m_ref.shape)
    return st_raw, inter, w_intra, w_inter, m_row


def _mlstm_chunk_output(vt, st_raw, inter, w_intra, w_inter, m_row):
    st = st_raw * w_intra
    num = w_inter * inter[:ML_V_DIM] + _dot(vt, st.astype(BF16))
    den = w_inter * inter[ML_V_DIM:ML_V_DIM + 1] + jnp.sum(st, axis=0, keepdims=True)
    ht = num / jnp.maximum(jnp.abs(den), jnp.exp(-m_row))
    return ht.T


def _mlstm_scan_kernel(qtf_ref, kf_ref, vtf_ref, grf_ref, ucf_ref,
                       qtb_ref, kb_ref, vtb_ref, grb_ref, ucb_ref,
                       hf_ref, hb_ref, state_ref, m_ref):
    @pl.when(pl.program_id(1) == 0)
    def _():
        state_ref[...] = jnp.zeros_like(state_ref)
        m_ref[...] = jnp.zeros_like(m_ref)

    n_chunks = kf_ref.shape[1] // ML_CHUNK
    fwd_refs = (qtf_ref, kf_ref, vtf_ref, grf_ref, ucf_ref, hf_ref)
    bwd_refs = (qtb_ref, kb_ref, vtb_ref, grb_ref, ucb_ref, hb_ref)

    def step(i, carry):
        pending = []
        for head in range(ML_HEADS):
            for reverse in (False, True):
                qt_ref, k_ref, vt_ref, gr_ref, uc_ref, h_ref = bwd_refs if reverse else fwd_refs
                c = n_chunks - 1 - i if reverse else i
                rows = pl.ds(pl.multiple_of(c * ML_CHUNK, ML_CHUNK), ML_CHUNK)
                qk_cols = slice(head * ML_QK_DIM, (head + 1) * ML_QK_DIM)
                v_cols = slice(head * ML_V_DIM, (head + 1) * ML_V_DIM)
                scan = int(reverse) * ML_HEADS + head
                vt = vt_ref[0, c, v_cols, :]
                parts = _mlstm_chunk_state(
                    qt_ref[0, c, qk_cols, :], k_ref[0, rows, qk_cols], vt,
                    gr_ref[0, c], uc_ref[0, rows, :], scan,
                    state_ref.at[scan], m_ref.at[scan], reverse)
                pending.append((h_ref, rows, v_cols, vt, parts))
        for h_ref, rows, v_cols, vt, parts in pending:
            h_ref[0, rows, v_cols] = _mlstm_chunk_output(vt, *parts)
        return carry

    lax.fori_loop(0, n_chunks, step, 0, unroll=2)


def _mlstm_scan(qt, k, vt, gr, uc, batch, seq):
    tm = TM_ML
    cpt = tm // ML_CHUNK
    nt = seq // tm
    n_chunks = seq // ML_CHUNK
    qt4 = qt.reshape(batch, n_chunks, ML_QK_COLS, ML_CHUNK)
    k3 = k.reshape(batch, seq, ML_QK_COLS)
    vt4 = vt.reshape(batch, n_chunks, D_MODEL, ML_CHUNK)
    gr4 = gr.reshape(batch, n_chunks, ML_GATE_COLS, ML_CHUNK)
    uc3 = uc.reshape(batch, seq, LANES)

    def specs(pos):
        return [pl.BlockSpec((1, cpt, ML_QK_COLS, ML_CHUNK), lambda b, i: (b, pos(i), 0, 0)),
                pl.BlockSpec((1, tm, ML_QK_COLS), lambda b, i: (b, pos(i), 0)),
                pl.BlockSpec((1, cpt, D_MODEL, ML_CHUNK), lambda b, i: (b, pos(i), 0, 0)),
                pl.BlockSpec((1, cpt, ML_GATE_COLS, ML_CHUNK), lambda b, i: (b, pos(i), 0, 0)),
                pl.BlockSpec((1, tm, LANES), lambda b, i: (b, pos(i), 0))]

    fwd = lambda i: i
    bwd = lambda i: nt - 1 - i
    h_shape = jax.ShapeDtypeStruct((batch, seq, D_MODEL), F32)
    return pl.pallas_call(
        _mlstm_scan_kernel,
        grid=(batch, nt),
        in_specs=specs(fwd) + specs(bwd),
        out_specs=[pl.BlockSpec((1, tm, D_MODEL), lambda b, i: (b, fwd(i), 0)),
                   pl.BlockSpec((1, tm, D_MODEL), lambda b, i: (b, bwd(i), 0))],
        out_shape=[h_shape, h_shape],
        scratch_shapes=[pltpu.VMEM((ML_SCANS, ML_STATE_ROWS, ML_QK_DIM), F32),
                        pltpu.VMEM((ML_SCANS, 8, LANES), F32)],
        compiler_params=_params("parallel", "arbitrary"),
        name="mlstm_scan",
    )(qt4, k3, vt4, gr4, uc3, qt4, k3, vt4, gr4, uc3)


def _row_blocks(tm):
    return [slice(r, r + FFN_SUB_ROWS) for r in range(0, tm, FFN_SUB_ROWS)]


def _ffn_tail(x_ref, mixer_out, g_ref, wgu_ref, wd_ref, out_ref):
    g1, g2, g3 = g_ref[0:1], g_ref[1:2], g_ref[2:3]
    blocks = _row_blocks(x_ref.shape[0])
    hs = [mixer_out(rows) for rows in blocks]
    x1s = [x_ref[rows, :] + _rms(h, g1) for rows, h in zip(blocks, hs)]
    xns = [_rms(x1, g2).astype(BF16) for x1 in x1s]
    ys = [None] * len(blocks)
    for lo, hi in FFN_SPLITS:
        acts = []
        for xn in xns:
            gate = _dot(xn, wgu_ref[:, lo:hi])
            up = _dot(xn, wgu_ref[:, FFN_HIDDEN + lo:FFN_HIDDEN + hi])
            acts.append((gate * jax.nn.sigmoid(gate) * up).astype(BF16))
        for s, act in enumerate(acts):
            part = _dot(act, wd_ref[lo:hi, :])
            ys[s] = part if ys[s] is None else ys[s] + part
    for rows, x1, y in zip(blocks, x1s, ys):
        out_ref[rows, :] = x1 + _rms(y, g3)


def _attn_post_kernel(x_ref, a_ref, wo_ref, g_ref, wgu_ref, wd_ref, out_ref):
    def mixer_out(rows):
        return _dot(a_ref[rows, :], wo_ref[...])

    _ffn_tail(x_ref, mixer_out, g_ref, wgu_ref, wd_ref, out_ref)


def _mlstm_post_kernel(x_ref, hf_ref, hb_ref, o_ref, hg_ref, wo_ref, g_ref, wgu_ref, wd_ref,
                       out_ref):
    def mixer_out(rows):
        hsum = hf_ref[rows, :] + hb_ref[rows, :]
        heads = []
        for hd in range(ML_HEADS):
            cols = slice(hd * ML_V_DIM, (hd + 1) * ML_V_DIM)
            heads.append(_rms(hsum[:, cols], hg_ref[:, cols]))
        a = (jnp.concatenate(heads, axis=1) * jax.nn.sigmoid(o_ref[rows, :])).astype(BF16)
        return _dot(a, wo_ref[...])

    _ffn_tail(x_ref, mixer_out, g_ref, wgu_ref, wd_ref, out_ref)


def _pool_post_kernel(x_ref, prev_ref, next_ref, g0_ref, wp_ref, bp_ref, sp_ref, g_ref,
                      wgu_ref, wd_ref, out_ref, ext_ref, *, tiles_per_seq):
    tm = x_ref.shape[0]
    hl = POOL_HALO
    pos = pl.program_id(0) % tiles_per_seq
    x = x_ref[...]
    g0 = g0_ref[...]
    hn = _rms(x, g0)
    ext_ref[0:hl, :] = jnp.where(pos > 0, _rms(prev_ref[...], g0), 0.0)
    ext_ref[hl:hl + tm, :] = hn
    ext_ref[hl + tm:, :] = jnp.where(pos < tiles_per_seq - 1, _rms(next_ref[...], g0), 0.0)
    seq = tiles_per_seq * tm

    def mixer_out(rows):
        r0, nr = rows.start, rows.stop - rows.start
        t = pos * tm + r0 + lax.broadcasted_iota(jnp.int32, (nr, 1), 0)
        parts = []
        for g, w in enumerate(POOL_WINDOWS):
            cols = slice(g * POOL_GROUP_DIM, (g + 1) * POOL_GROUP_DIM)
            base = hl + r0
            tot = ext_ref[base - w // 2:base - w // 2 + nr, cols]
            for off in range(-w // 2 + 1, w // 2):
                tot = tot + ext_ref[base + off:base + off + nr, cols]
            cnt = (jnp.minimum(t + w // 2, seq) - jnp.maximum(t - w // 2, 0)).astype(F32)
            d = (tot / cnt - ext_ref[base:base + nr, cols]).astype(BF16)
            parts.append(_dot(d, wp_ref[g]))
        return (jnp.concatenate(parts, axis=1) + bp_ref[...]) * sp_ref[...]

    _ffn_tail(x_ref, mixer_out, g_ref, wgu_ref, wd_ref, out_ref)


def _ffn_specs():
    return [_const_spec((3, D_MODEL)), _const_spec((D_MODEL, 2 * FFN_HIDDEN)),
            _const_spec((FFN_HIDDEN, D_MODEL))]


def _attn_post(x2d, a, wo, gains, wgu, wd):
    t = x2d.shape[0]
    row = pl.BlockSpec((TM_FFN, D_MODEL), lambda i: (i, 0))
    return pl.pallas_call(
        _attn_post_kernel,
        grid=(t // TM_FFN,),
        in_specs=[row, row, _const_spec((D_MODEL, D_MODEL))] + _ffn_specs(),
        out_specs=row,
        out_shape=jax.ShapeDtypeStruct((t, D_MODEL), F32),
        compiler_params=_params("parallel"),
        name="attn_post_ffn",
    )(x2d, a, wo, gains, wgu, wd)


def _mlstm_post(x2d, hf, hb, o, head_gain, wo, gains, wgu, wd):
    t = x2d.shape[0]
    row = pl.BlockSpec((TM_FFN, D_MODEL), lambda i: (i, 0))
    return pl.pallas_call(
        _mlstm_post_kernel,
        grid=(t // TM_FFN,),
        in_specs=[row, row, row, row, _const_spec((1, D_MODEL)),
                  _const_spec((D_MODEL, D_MODEL))] + _ffn_specs(),
        out_specs=row,
        out_shape=jax.ShapeDtypeStruct((t, D_MODEL), F32),
        compiler_params=_params("parallel"),
        name="mlstm_post_ffn",
    )(x2d, hf, hb, o, head_gain, wo, gains, wgu, wd)


def _pool_post(x2d, g0, wp, bp, sp, gains, wgu, wd, seq):
    t = x2d.shape[0]
    tm = TM_FFN
    tps = seq // tm
    hpt = tm // POOL_HALO
    n_halo = t // POOL_HALO
    row = pl.BlockSpec((tm, D_MODEL), lambda i: (i, 0))
    prev = pl.BlockSpec((POOL_HALO, D_MODEL), lambda i: (jnp.maximum(i * hpt - 1, 0), 0))
    nxt = pl.BlockSpec((POOL_HALO, D_MODEL),
                       lambda i: (jnp.minimum((i + 1) * hpt, n_halo - 1), 0))
    kern = functools.partial(_pool_post_kernel, tiles_per_seq=tps)
    return pl.pallas_call(
        kern,
        grid=(t // tm,),
        in_specs=[row, prev, nxt, _const_spec((1, D_MODEL)),
                  _const_spec((POOL_GROUPS, POOL_GROUP_DIM, POOL_GROUP_DIM)),
                  _const_spec((1, D_MODEL)), _const_spec((1, D_MODEL))] + _ffn_specs(),
        out_specs=row,
        out_shape=jax.ShapeDtypeStruct((t, D_MODEL), F32),
        scratch_shapes=[pltpu.VMEM((tm + 2 * POOL_HALO, D_MODEL), F32)],
        compiler_params=_params("parallel"),
        name="pool_post_ffn",
    )(x2d, x2d, x2d, g0, wp, bp, sp, gains, wgu, wd)


def _rope_tables(seq):
    inv = ROPE_THETA ** (-jnp.arange(0, ROPE_DIM, 2, dtype=F32) / ROPE_DIM)
    ang = jnp.arange(seq, dtype=F32)[:, None] * inv[None, :]
    cos, sin = jnp.cos(ang), jnp.sin(ang)
    pad = DA_QK_DIM - ROPE_DIM
    cos_c = jnp.concatenate([cos, cos, jnp.ones((seq, pad), F32)], axis=1)
    sin_c = jnp.concatenate([-sin, sin, jnp.zeros((seq, pad), F32)], axis=1)
    return jnp.tile(cos_c, (1, 2)), jnp.tile(sin_c, (1, 2))


def _lambda_init(layer_idx):
    return 0.8 - 0.6 * math.exp(-0.3 * layer_idx)


def _trunk(x, w):
    batch, seq, _ = x.shape
    x2d = x.reshape(batch * seq, D_MODEL)
    cos_t, sin_t = _rope_tables(seq)
    for i in range(DEPTH):
        kind, slot = i % N_MIXERS, i // N_MIXERS
        g0 = w["norm_gain"][i, 0:1]
        gains = w["norm_gain"][i, 1:4]
        wgu, wd = w["w_ffn_gate_up"][i], w["w_ffn_down"][i]
        if kind == 0:
            q, k, vt = _attn_proj(x2d, g0, w["wq"][slot], w["wk"][slot], w["wvt"][slot],
                                  cos_t, sin_t, seq)
            a = _diff_attn(q, k, vt, w["attn_lambda"][slot], w["subln_col"][slot],
                           batch, seq, _lambda_init(i))
            x2d = _attn_post(x2d, a, w["w_attn_out"][slot], gains, wgu, wd)
        elif kind == 1:
            x2d = _pool_post(x2d, g0, w["w_pool"][slot], w["b_pool"][slot],
                             w["pool_scale"][slot], gains, wgu, wd, seq)
        else:
            qt, k, vt, o, gr, uc = _mlstm_proj(
                x2d, g0, w["ml_wqt"][slot], w["ml_wk"][slot], w["ml_wvt"][slot],
                w["ml_wo"][slot], w["ml_wgt"][slot], w["ml_bgt"][slot])
            hf, hb = _mlstm_scan(qt, k, vt, gr, uc, batch, seq)
            x2d = _mlstm_post(x2d, hf.reshape(-1, D_MODEL), hb.reshape(-1, D_MODEL), o,
                              w["ml_head_gain"][slot], w["w_mlstm_out"][slot], gains, wgu, wd)
    return x2d.reshape(batch, seq, D_MODEL)


def _prepare_weights(norm_gain, w_attn_in, attn_lambda, attn_subln_gain, w_attn_out, w_pool,
                     b_pool, pool_scale, w_mlstm_in, b_mlstm_gate, mlstm_head_gain,
                     w_mlstm_out, w_ffn_gate_up, w_ffn_down):
    d, qk = D_MODEL, ML_QK_COLS
    swap = lambda a: jnp.swapaxes(a, -1, -2)
    wm = w_mlstm_in
    wg = wm[:, :, 2 * qk + 2 * d:]
    nh = ML_HEADS
    gate_order = jnp.array([kind * nh + h for kind in (1, 3, 0, 2) for h in range(nh)])
    return {
        "norm_gain": norm_gain,
        "wq": w_attn_in[:, :, :d].astype(BF16),
        "wk": w_attn_in[:, :, d:2 * d].astype(BF16),
        "wvt": swap(w_attn_in[:, :, 2 * d:]).astype(BF16),
        "attn_lambda": attn_lambda,
        "subln_col": attn_subln_gain[:, :, None],
        "w_attn_out": w_attn_out.astype(BF16),
        "w_pool": w_pool.astype(BF16),
        "b_pool": b_pool[:, None, :],
        "pool_scale": pool_scale[:, None, :],
        "ml_wqt": swap(wm[:, :, :qk]).astype(BF16),
        "ml_wk": wm[:, :, qk:2 * qk].astype(BF16),
        "ml_wvt": swap(wm[:, :, 2 * qk:2 * qk + d]).astype(BF16),
        "ml_wo": wm[:, :, 2 * qk + d:2 * qk + 2 * d].astype(BF16),
        "ml_wgt": swap(wg[:, :, gate_order]).astype(BF16),
        "ml_bgt": b_mlstm_gate[:, gate_order, None],
        "ml_head_gain": mlstm_head_gain[:, None, :],
        "w_mlstm_out": w_mlstm_out.astype(BF16),
        "w_ffn_gate_up": w_ffn_gate_up.astype(BF16),
        "w_ffn_down": w_ffn_down.astype(BF16),
    }


def kernel(x_prompt, x_sample, norm_gain, w_attn_in, attn_lambda, attn_subln_gain, w_attn_out,
           w_pool, b_pool, pool_scale, w_mlstm_in, b_mlstm_gate, mlstm_head_gain, w_mlstm_out,
           w_ffn_gate_up, w_ffn_down):
    w = _prepare_weights(norm_gain, w_attn_in, attn_lambda, attn_subln_gain, w_attn_out,
                         w_pool, b_pool, pool_scale, w_mlstm_in, b_mlstm_gate,
                         mlstm_head_gain, w_mlstm_out, w_ffn_gate_up, w_ffn_down)
    return _trunk(x_prompt, w), _trunk(x_sample, w)
```

```python
import functools
import math

import jax
import jax.numpy as jnp
from jax import lax
from jax.experimental import pallas as pl
from jax.experimental.pallas import tpu as pltpu

D_MODEL = 1024
DEPTH = 4
N_MIXERS = 3
DA_HEADS = 8
DA_QK_DIM = 64
DA_V_DIM = 128
ROPE_DIM = 16
ROPE_THETA = 500000.0
POOL_GROUPS = 4
POOL_WINDOWS = (2, 4, 8, 16)
POOL_GROUP_DIM = D_MODEL // POOL_GROUPS
POOL_HALO = 8
ML_HEADS = 4
ML_QK_DIM = 128
ML_V_DIM = 256
ML_CHUNK = 128
ML_QK_COLS = ML_HEADS * ML_QK_DIM
ML_GATE_COLS = 4 * ML_HEADS
FFN_HIDDEN = 2816
NORM_EPS = 1e-6
LOG2E = math.log2(math.e)

LANES = 128
VMEM_LIMIT = 56 * 1024 * 1024
TM_PROJ = 512
TM_FFN = 512
FFN_SUB_ROWS = 256
FFN_SPLITS = ((0, 1536), (1536, FFN_HIDDEN))
TQ_ATTN = 256
TM_ML = 512

BF16 = jnp.bfloat16
F32 = jnp.float32


def _params(*sem):
    return pltpu.CompilerParams(dimension_semantics=sem, vmem_limit_bytes=VMEM_LIMIT)


def _const_spec(shape):
    nd = len(shape)
    return pl.BlockSpec(shape, lambda *_: (0,) * nd, pipeline_mode=pl.Buffered(1))


def _rms(x, gain):
    ms = jnp.mean(x * x, axis=-1, keepdims=True)
    return x * lax.rsqrt(ms + NORM_EPS) * gain


def _dot(a, b):
    return jnp.dot(a, b, preferred_element_type=F32)


def _dot_nt(a, b):
    return lax.dot_general(a, b, (((1,), (1,)), ((), ())), preferred_element_type=F32)


def _attn_proj_kernel(x_ref, g_ref, wq_ref, wk_ref, wvt_ref, cos_ref, sin_ref,
                      q_ref, k_ref, vt_ref):
    xn = _rms(x_ref[...], g_ref[...]).astype(BF16)
    cos = cos_ref[...]
    sin = sin_ref[...]
    lane = lax.broadcasted_iota(jnp.int32, (1, LANES), 1) % DA_QK_DIM
    first_half = lane < ROPE_DIM // 2

    def rope(y, scale):
        outs = []
        for h in range(DA_HEADS):
            slab = y[:, h * LANES:(h + 1) * LANES]
            partner = jnp.where(first_half,
                                pltpu.roll(slab, LANES - ROPE_DIM // 2, 1),
                                pltpu.roll(slab, ROPE_DIM // 2, 1))
            outs.append(((slab * cos + partner * sin) * scale).astype(BF16))
        return jnp.concatenate(outs, axis=1)

    q_ref[...] = rope(_dot(xn, wq_ref[...]), DA_QK_DIM ** -0.5 * LOG2E)
    k_ref[...] = rope(_dot(xn, wk_ref[...]), 1.0)
    vt_ref[0] = _dot_nt(wvt_ref[...], xn).astype(BF16)


def _attn_proj(x2d, gain, wq, wk, wvt, cos_t, sin_t, seq):
    t = x2d.shape[0]
    tm = TM_PROJ
    nps = seq // tm
    row = pl.BlockSpec((tm, D_MODEL), lambda i: (i, 0))
    tab = pl.BlockSpec((tm, LANES), lambda i: (i % nps, 0))
    return pl.pallas_call(
        _attn_proj_kernel,
        grid=(t // tm,),
        in_specs=[row, _const_spec((1, D_MODEL)), _const_spec((D_MODEL, D_MODEL)),
                  _const_spec((D_MODEL, D_MODEL)), _const_spec((D_MODEL, D_MODEL)), tab, tab],
        out_specs=[row, row, pl.BlockSpec((1, D_MODEL, tm), lambda i: (i, 0, 0))],
        out_shape=[jax.ShapeDtypeStruct((t, D_MODEL), BF16),
                   jax.ShapeDtypeStruct((t, D_MODEL), BF16),
                   jax.ShapeDtypeStruct((t // tm, D_MODEL, tm), BF16)],
        compiler_params=_params("parallel"),
        name="attn_proj",
    )(x2d, gain, wq, wk, wvt, cos_t, sin_t)


DV_AUG = DA_V_DIM + 8


def _diff_attn_kernel(q_ref, k_ref, vt_ref, lam_ref, gain_ref, o_ref, acc_ref, fin_ref,
                      sa_ref, sb_ref, *, n_kv, tk, tq, unroll, q_unroll, lambda_init):
    n_q = q_ref.shape[1] // tq
    n_pairs = n_kv // 2
    lane = lax.broadcasted_iota(jnp.int32, (1, LANES), 1)
    lp = lam_ref[...]
    lam = (jnp.exp(jnp.sum(lp[0:1] * lp[1:2], keepdims=True))
           - jnp.exp(jnp.sum(lp[2:3] * lp[3:4], keepdims=True)) + lambda_init)
    out_gain = gain_ref[...] * (1.0 - lambda_init)
    neg = jnp.full((1, tq), -jnp.inf, F32)

    def masked_q(qi):
        q = q_ref[0, pl.ds(pl.multiple_of(qi * tq, tq), tq), :]
        zero = jnp.zeros_like(q)
        return jnp.where(lane < DA_QK_DIM, q, zero), jnp.where(lane >= DA_QK_DIM, q, zero)

    def scores(qm, j, dst_ref):
        kc = k_ref[0, pl.ds(pl.multiple_of(j * tk, tk), tk), :]
        for c in range(2):
            dst_ref[c] = _dot_nt(kc, qm[c])

    def consume(j, src_ref, m):
        vt = vt_ref[0, j]
        m_out = []
        for c in range(2):
            s = src_ref[c]
            m_new = jnp.maximum(m[c], jnp.max(s, axis=0, keepdims=True))
            alpha = jnp.exp2(m[c] - m_new)
            p = jnp.exp2(s - m_new)
            acc_ref[c, DA_V_DIM:, :] = (alpha * acc_ref[c, DA_V_DIM:, :]
                                        + jnp.sum(p, axis=0, keepdims=True))
            acc_ref[c, :DA_V_DIM, :] = (alpha * acc_ref[c, :DA_V_DIM, :]
                                        + _dot(vt, p.astype(BF16)))
            m_out.append(m_new)
        return tuple(m_out)

    def finalize(qi):
        a1, a2 = fin_ref[0], fin_ref[1]
        o = (a1[:DA_V_DIM] / a1[DA_V_DIM:DA_V_DIM + 1]
             - lam * (a2[:DA_V_DIM] / a2[DA_V_DIM:DA_V_DIM + 1]))
        ms = jnp.mean(o * o, axis=0, keepdims=True)
        y = o * lax.rsqrt(ms + NORM_EPS) * out_gain
        start = qi * tq if isinstance(qi, int) else pl.multiple_of(qi * tq, tq)
        o_ref[0, pl.ds(start, tq), :] = y.T.astype(BF16)

    def q_tile(qi, carry):
        qm = masked_q(qi)
        qm_next = masked_q(jnp.minimum(qi + 1, n_q - 1))
        finalize(jnp.maximum(qi - 1, 0))

        def pair(i, m):
            j = 2 * i
            scores(qm, j + 1, sb_ref)
            m = consume(j, sa_ref, m)
            scores(qm, j + 2, sa_ref)
            return consume(j + 1, sb_ref, m)

        m = lax.fori_loop(0, n_pairs - 1, pair, (neg, neg), unroll=unroll)
        scores(qm, n_kv - 1, sb_ref)
        m = consume(n_kv - 2, sa_ref, m)
        scores(qm_next, 0, sa_ref)
        consume(n_kv - 1, sb_ref, m)
        fin_ref[...] = acc_ref[...]
        return carry

    acc_ref[...] = jnp.zeros_like(acc_ref)
    fin_ref[...] = jnp.ones_like(fin_ref)
    scores(masked_q(0), 0, sa_ref)
    lax.fori_loop(0, n_q, q_tile, 0, unroll=q_unroll)
    finalize(n_q - 1)


def _diff_attn(q, k, vt, lam_params, gain_col, batch, seq, lambda_init):
    tq, tk = TQ_ATTN, TM_PROJ
    n_kv = seq // tk
    assert n_kv % 2 == 0 and seq % tq == 0
    n_loop = n_kv // 2 - 1
    unroll = True if n_loop <= 1 else (5 if n_loop % 5 == 0 else 1)
    q3 = q.reshape(batch, seq, D_MODEL)
    k3 = k.reshape(batch, seq, D_MODEL)
    vt4 = vt.reshape(batch, n_kv, D_MODEL, tk)
    q_unroll = 4 if n_loop <= 1 and (seq // tq) % 4 == 0 else 1
    kern = functools.partial(_diff_attn_kernel, n_kv=n_kv, tk=tk, tq=tq, unroll=unroll,
                             q_unroll=q_unroll, lambda_init=lambda_init)
    head = pl.BlockSpec((1, seq, LANES), lambda b, h: (b, 0, h))
    out = pl.pallas_call(
        kern,
        grid=(batch, DA_HEADS),
        in_specs=[head, head,
                  pl.BlockSpec((1, n_kv, DA_V_DIM, tk), lambda b, h: (b, 0, h, 0)),
                  pl.BlockSpec((4, DA_QK_DIM), lambda b, h: (0, 0)),
                  pl.BlockSpec((DA_V_DIM, 1), lambda b, h: (0, 0))],
        out_specs=head,
        out_shape=jax.ShapeDtypeStruct((batch, seq, D_MODEL), BF16),
        scratch_shapes=[pltpu.VMEM((2, DV_AUG, tq), F32), pltpu.VMEM((2, DV_AUG, tq), F32),
                        pltpu.VMEM((2, tk, tq), F32), pltpu.VMEM((2, tk, tq), F32)],
        compiler_params=_params("parallel", "parallel"),
        name="diff_attn",
    )(q3, k3, vt4, lam_params, gain_col)
    return out.reshape(batch * seq, D_MODEL)


ML_SCANS = 2 * ML_HEADS


def _mlstm_proj_kernel(x_ref, g_ref, wqt_ref, wk_ref, wvt_ref, wo_ref, wgt_ref, bgt_ref,
                       qt_ref, k_ref, vt_ref, o_ref, gr_ref, uc_ref):
    tm = x_ref.shape[0]
    xn = _rms(x_ref[...], g_ref[...]).astype(BF16)

    gates = _dot_nt(wgt_ref[...], xn) + bgt_ref[...]
    fg = gates[:ML_SCANS]
    log_f = jnp.minimum(fg, 0.0) - jnp.log1p(jnp.exp(-jnp.abs(fg)))
    pos = lax.broadcasted_iota(jnp.int32, (1, tm), 1) % ML_CHUNK
    pre = suf = log_f
    shift = 1
    while shift < ML_CHUNK:
        pre = pre + jnp.where(pos >= shift, pltpu.roll(pre, shift, 1), 0.0)
        suf = suf + jnp.where(pos < ML_CHUNK - shift, pltpu.roll(suf, tm - shift, 1), 0.0)
        shift *= 2
    is_fwd = lax.broadcasted_iota(jnp.int32, (ML_SCANS, 1), 0) < ML_HEADS
    b = jnp.where(is_fwd, pre, suf)
    u = gates[ML_SCANS:] - b
    gr = jnp.concatenate([b, u], axis=0)
    uc_ref[...] = jnp.concatenate([u, jnp.zeros((LANES - ML_SCANS, tm), F32)], axis=0).T

    qt = _dot_nt(wqt_ref[...], xn).astype(BF16)
    k_ref[...] = (_dot(xn, wk_ref[...]) * ML_QK_DIM ** -0.5).astype(BF16)
    vt = _dot_nt(wvt_ref[...], xn).astype(BF16)
    o_ref[...] = _dot(xn, wo_ref[...])
    for c in range(tm // ML_CHUNK):
        cols = slice(c * ML_CHUNK, (c + 1) * ML_CHUNK)
        qt_ref[0, c] = qt[:, cols]
        vt_ref[0, c] = vt[:, cols]
        gr_ref[0, c] = gr[:, cols]


def _mlstm_proj(x2d, gain, wqt, wk, wvt, wo, wgt, bgt):
    t = x2d.shape[0]
    tm = TM_PROJ
    cpt = tm // ML_CHUNK
    row = lambda n: pl.BlockSpec((tm, n), lambda i: (i, 0))
    chunks = lambda n: pl.BlockSpec((1, cpt, n, ML_CHUNK), lambda i: (i, 0, 0, 0))
    chunk_shape = lambda n, dt: jax.ShapeDtypeStruct((t // tm, cpt, n, ML_CHUNK), dt)
    return pl.pallas_call(
        _mlstm_proj_kernel,
        grid=(t // tm,),
        in_specs=[row(D_MODEL), _const_spec((1, D_MODEL)),
                  _const_spec((ML_QK_COLS, D_MODEL)), _const_spec((D_MODEL, ML_QK_COLS)),
                  _const_spec((D_MODEL, D_MODEL)), _const_spec((D_MODEL, D_MODEL)),
                  _const_spec((ML_GATE_COLS, D_MODEL)), _const_spec((ML_GATE_COLS, 1))],
        out_specs=[chunks(ML_QK_COLS), row(ML_QK_COLS), chunks(D_MODEL), row(D_MODEL),
                   chunks(ML_GATE_COLS), row(LANES)],
        out_shape=[chunk_shape(ML_QK_COLS, BF16),
                   jax.ShapeDtypeStruct((t, ML_QK_COLS), BF16),
                   chunk_shape(D_MODEL, BF16),
                   jax.ShapeDtypeStruct((t, D_MODEL), F32),
                   chunk_shape(ML_GATE_COLS, F32),
                   jax.ShapeDtypeStruct((t, LANES), F32)],
        compiler_params=_params("parallel"),
        name="mlstm_proj",
    )(x2d, gain, wqt, wk, wvt, wo, wgt, bgt)


ML_STATE_ROWS = ML_V_DIM + 16


def _mlstm_chunk_state(qt, k, vt, gr, uc, scan, state_ref, m_ref, reverse):
    L = ML_CHUNK
    b_row = gr[scan:scan + 1]
    u_row = gr[ML_SCANS + scan:ML_SCANS + scan + 1]
    u_col = uc[:, scan:scan + 1]
    b_last = b_row[:, 0:1] if reverse else b_row[:, L - 1:L]
    s_idx = lax.broadcasted_iota(jnp.int32, (L, L), 0)
    j_idx = lax.broadcasted_iota(jnp.int32, (L, L), 1)
    keep = (s_idx >= j_idx) if reverse else (s_idx <= j_idx)
    m_old = m_ref[0:1, 0:1]

    dlog = jnp.where(keep, u_col + b_row, -jnp.inf)
    g = b_row + m_old
    m_row = jnp.maximum(g, jnp.max(dlog, axis=0, keepdims=True))
    w_intra = jnp.exp(dlog - m_row)
    w_inter = jnp.exp(g - m_row)

    st_raw = _dot(k, qt)
    state = state_ref[...]
    inter = _dot(state.astype(BF16), qt)

    w_state_log = b_last + u_row
    m_new = jnp.maximum(b_last + m_old, jnp.max(w_state_log, axis=1, keepdims=True))
    w_state = jnp.exp(w_state_log - m_new)
    decay = jnp.exp(b_last + m_old - m_new)
    vt_aug = jnp.concatenate(
        [vt.astype(F32), jnp.ones((ML_STATE_ROWS - ML_V_DIM, L), F32)], axis=0)
    state_ref[...] = decay * state + _dot((vt_aug * w_state).astype(BF16), k)
    m_ref[...] = jnp.broadcast_to(m_new, m_ref.shape)
    return st_raw, inter, w_intra, w_inter, m_row


def _mlstm_chunk_output(vt, st_raw, inter, w_intra, w_inter, m_row):
    st = st_raw * w_intra
    num = w_inter * inter[:ML_V_DIM] + _dot(vt, st.astype(BF16))
    den = w_inter * inter[ML_V_DIM:ML_V_DIM + 1] + jnp.sum(st, axis=0, keepdims=True)
    ht = num / jnp.maximum(jnp.abs(den), jnp.exp(-m_row))
    return ht.T


def _mlstm_scan_kernel(qtf_ref, kf_ref, vtf_ref, grf_ref, ucf_ref,
                       qtb_ref, kb_ref, vtb_ref, grb_ref, ucb_ref,
                       hf_ref, hb_ref, state_ref, m_ref):
    @pl.when(pl.program_id(1) == 0)
    def _():
        state_ref[...] = jnp.zeros_like(state_ref)
        m_ref[...] = jnp.zeros_like(m_ref)

    n_chunks = kf_ref.shape[1] // ML_CHUNK
    fwd_refs = (qtf_ref, kf_ref, vtf_ref, grf_ref, ucf_ref, hf_ref)
    bwd_refs = (qtb_ref, kb_ref, vtb_ref, grb_ref, ucb_ref, hb_ref)

    def step(i, carry):
        pending = []
        for head in range(ML_HEADS):
            for reverse in (False, True):
                qt_ref, k_ref, vt_ref, gr_ref, uc_ref, h_ref = bwd_refs if reverse else fwd_refs
                c = n_chunks - 1 - i if reverse else i
                rows = pl.ds(pl.multiple_of(c * ML_CHUNK, ML_CHUNK), ML_CHUNK)
                qk_cols = slice(head * ML_QK_DIM, (head + 1) * ML_QK_DIM)
                v_cols = slice(head * ML_V_DIM, (head + 1) * ML_V_DIM)
                scan = int(reverse) * ML_HEADS + head
                vt = vt_ref[0, c, v_cols, :]
                parts = _mlstm_chunk_state(
                    qt_ref[0, c, qk_cols, :], k_ref[0, rows, qk_cols], vt,
                    gr_ref[0, c], uc_ref[0, rows, :], scan,
                    state_ref.at[scan], m_ref.at[scan], reverse)
                pending.append((h_ref, rows, v_cols, vt, parts))
        for h_ref, rows, v_cols, vt, parts in pending:
            h_ref[0, rows, v_cols] = _mlstm_chunk_output(vt, *parts)
        return carry

    lax.fori_loop(0, n_chunks, step, 0, unroll=2)


def _mlstm_scan(qt, k, vt, gr, uc, batch, seq):
    tm = TM_ML
    cpt = tm // ML_CHUNK
    nt = seq // tm
    n_chunks = seq // ML_CHUNK
    qt4 = qt.reshape(batch, n_chunks, ML_QK_COLS, ML_CHUNK)
    k3 = k.reshape(batch, seq, ML_QK_COLS)
    vt4 = vt.reshape(batch, n_chunks, D_MODEL, ML_CHUNK)
    gr4 = gr.reshape(batch, n_chunks, ML_GATE_COLS, ML_CHUNK)
    uc3 = uc.reshape(batch, seq, LANES)

    def specs(pos):
        return [pl.BlockSpec((1, cpt, ML_QK_COLS, ML_CHUNK), lambda b, i: (b, pos(i), 0, 0)),
                pl.BlockSpec((1, tm, ML_QK_COLS), lambda b, i: (b, pos(i), 0)),
                pl.BlockSpec((1, cpt, D_MODEL, ML_CHUNK), lambda b, i: (b, pos(i), 0, 0)),
                pl.BlockSpec((1, cpt, ML_GATE_COLS, ML_CHUNK), lambda b, i: (b, pos(i), 0, 0)),
                pl.BlockSpec((1, tm, LANES), lambda b, i: (b, pos(i), 0))]

    fwd = lambda i: i
    bwd = lambda i: nt - 1 - i
    h_shape = jax.ShapeDtypeStruct((batch, seq, D_MODEL), F32)
    return pl.pallas_call(
        _mlstm_scan_kernel,
        grid=(batch, nt),
        in_specs=specs(fwd) + specs(bwd),
        out_specs=[pl.BlockSpec((1, tm, D_MODEL), lambda b, i: (b, fwd(i), 0)),
                   pl.BlockSpec((1, tm, D_MODEL), lambda b, i: (b, bwd(i), 0))],
        out_shape=[h_shape, h_shape],
        scratch_shapes=[pltpu.VMEM((ML_SCANS, ML_STATE_ROWS, ML_QK_DIM), F32),
                        pltpu.VMEM((ML_SCANS, 8, LANES), F32)],
        compiler_params=_params("parallel", "arbitrary"),
        name="mlstm_scan",
    )(qt4, k3, vt4, gr4, uc3, qt4, k3, vt4, gr4, uc3)


def _row_blocks(tm):
    return [slice(r, r + FFN_SUB_ROWS) for r in range(0, tm, FFN_SUB_ROWS)]


def _ffn_tail(x_ref, mixer_out, g_ref, wgu_ref, wd_ref, out_ref):
    g1, g2, g3 = g_ref[0:1], g_ref[1:2], g_ref[2:3]
    blocks = _row_blocks(x_ref.shape[0])
    hs = [mixer_out(rows) for rows in blocks]
    x1s = [x_ref[rows, :] + _rms(h, g1) for rows, h in zip(blocks, hs)]
    xns = [_rms(x1, g2).astype(BF16) for x1 in x1s]
    ys = [None] * len(blocks)
    for lo, hi in FFN_SPLITS:
        acts = []
        for xn in xns:
            gate = _dot(xn, wgu_ref[:, lo:hi])
            up = _dot(xn, wgu_ref[:, FFN_HIDDEN + lo:FFN_HIDDEN + hi])
            acts.append((gate * jax.nn.sigmoid(gate) * up).astype(BF16))
        for s, act in enumerate(acts):
            part = _dot(act, wd_ref[lo:hi, :])
            ys[s] = part if ys[s] is None else ys[s] + part
    for rows, x1, y in zip(blocks, x1s, ys):
        out_ref[rows, :] = x1 + _rms(y, g3)


def _attn_post_kernel(x_ref, a_ref, wo_ref, g_ref, wgu_ref, wd_ref, out_ref):
    def mixer_out(rows):
        return _dot(a_ref[rows, :], wo_ref[...])

    _ffn_tail(x_ref, mixer_out, g_ref, wgu_ref, wd_ref, out_ref)


def _mlstm_post_kernel(x_ref, hf_ref, hb_ref, o_ref, hg_ref, wo_ref, g_ref, wgu_ref, wd_ref,
                       out_ref):
    def mixer_out(rows):
        hsum = hf_ref[rows, :] + hb_ref[rows, :]
        heads = []
        for hd in range(ML_HEADS):
            cols = slice(hd * ML_V_DIM, (hd + 1) * ML_V_DIM)
            heads.append(_rms(hsum[:, cols], hg_ref[:, cols]))
        a = (jnp.concatenate(heads, axis=1) * jax.nn.sigmoid(o_ref[rows, :])).astype(BF16)
        return _dot(a, wo_ref[...])

    _ffn_tail(x_ref, mixer_out, g_ref, wgu_ref, wd_ref, out_ref)


def _pool_post_kernel(x_ref, prev_ref, next_ref, g0_ref, wp_ref, bp_ref, sp_ref, g_ref,
                      wgu_ref, wd_ref, out_ref, ext_ref, *, tiles_per_seq):
    tm = x_ref.shape[0]
    hl = POOL_HALO
    pos = pl.program_id(0) % tiles_per_seq
    x = x_ref[...]
    g0 = g0_ref[...]
    hn = _rms(x, g0)
    ext_ref[0:hl, :] = jnp.where(pos > 0, _rms(prev_ref[...], g0), 0.0)
    ext_ref[hl:hl + tm, :] = hn
    ext_ref[hl + tm:, :] = jnp.where(pos < tiles_per_seq - 1, _rms(next_ref[...], g0), 0.0)
    seq = tiles_per_seq * tm

    def mixer_out(rows):
        r0, nr = rows.start, rows.stop - rows.start
        t = pos * tm + r0 + lax.broadcasted_iota(jnp.int32, (nr, 1), 0)
        parts = []
        for g, w in enumerate(POOL_WINDOWS):
            cols = slice(g * POOL_GROUP_DIM, (g + 1) * POOL_GROUP_DIM)
            base = hl + r0
            tot = ext_ref[base - w // 2:base - w // 2 + nr, cols]
            for off in range(-w // 2 + 1, w // 2):
                tot = tot + ext_ref[base + off:base + off + nr, cols]
            cnt = (jnp.minimum(t + w // 2, seq) - jnp.maximum(t - w // 2, 0)).astype(F32)
            d = (tot / cnt - ext_ref[base:base + nr, cols]).astype(BF16)
            parts.append(_dot(d, wp_ref[g]))
        return (jnp.concatenate(parts, axis=1) + bp_ref[...]) * sp_ref[...]

    _ffn_tail(x_ref, mixer_out, g_ref, wgu_ref, wd_ref, out_ref)


def _ffn_specs():
    return [_const_spec((3, D_MODEL)), _const_spec((D_MODEL, 2 * FFN_HIDDEN)),
            _const_spec((FFN_HIDDEN, D_MODEL))]


def _attn_post(x2d, a, wo, gains, wgu, wd):
    t = x2d.shape[0]
    row = pl.BlockSpec((TM_FFN, D_MODEL), lambda i: (i, 0))
    return pl.pallas_call(
        _attn_post_kernel,
        grid=(t // TM_FFN,),
        in_specs=[row, row, _const_spec((D_MODEL, D_MODEL))] + _ffn_specs(),
        out_specs=row,
        out_shape=jax.ShapeDtypeStruct((t, D_MODEL), F32),
        compiler_params=_params("parallel"),
        name="attn_post_ffn",
    )(x2d, a, wo, gains, wgu, wd)


def _mlstm_post(x2d, hf, hb, o, head_gain, wo, gains, wgu, wd):
    t = x2d.shape[0]
    row = pl.BlockSpec((TM_FFN, D_MODEL), lambda i: (i, 0))
    return pl.pallas_call(
        _mlstm_post_kernel,
        grid=(t // TM_FFN,),
        in_specs=[row, row, row, row, _const_spec((1, D_MODEL)),
                  _const_spec((D_MODEL, D_MODEL))] + _ffn_specs(),
        out_specs=row,
        out_shape=jax.ShapeDtypeStruct((t, D_MODEL), F32),
        compiler_params=_params("parallel"),
        name="mlstm_post_ffn",
    )(x2d, hf, hb, o, head_gain, wo, gains, wgu, wd)


def _pool_post(x2d, g0, wp, bp, sp, gains, wgu, wd, seq):
    t = x2d.shape[0]
    tm = TM_FFN
    tps = seq // tm
    hpt = tm // POOL_HALO
    n_halo = t // POOL_HALO
    row = pl.BlockSpec((tm, D_MODEL), lambda i: (i, 0))
    prev = pl.BlockSpec((POOL_HALO, D_MODEL), lambda i: (jnp.maximum(i * hpt - 1, 0), 0))
    nxt = pl.BlockSpec((POOL_HALO, D_MODEL),
                       lambda i: (jnp.minimum((i + 1) * hpt, n_halo - 1), 0))
    kern = functools.partial(_pool_post_kernel, tiles_per_seq=tps)
    return pl.pallas_call(
        kern,
        grid=(t // tm,),
        in_specs=[row, prev, nxt, _const_spec((1, D_MODEL)),
                  _const_spec((POOL_GROUPS, POOL_GROUP_DIM, POOL_GROUP_DIM)),
                  _const_spec((1, D_MODEL)), _const_spec((1, D_MODEL))] + _ffn_specs(),
        out_specs=row,
        out_shape=jax.ShapeDtypeStruct((t, D_MODEL), F32),
        scratch_shapes=[pltpu.VMEM((tm + 2 * POOL_HALO, D_MODEL), F32)],
        compiler_params=_params("parallel"),
        name="pool_post_ffn",
    )(x2d, x2d, x2d, g0, wp, bp, sp, gains, wgu, wd)


def _rope_tables(seq):
    inv = ROPE_THETA ** (-jnp.arange(0, ROPE_DIM, 2, dtype=F32) / ROPE_DIM)
    ang = jnp.arange(seq, dtype=F32)[:, None] * inv[None, :]
    cos, sin = jnp.cos(ang), jnp.sin(ang)
    pad = DA_QK_DIM - ROPE_DIM
    cos_c = jnp.concatenate([cos, cos, jnp.ones((seq, pad), F32)], axis=1)
    sin_c = jnp.concatenate([-sin, sin, jnp.zeros((seq, pad), F32)], axis=1)
    return jnp.tile(cos_c, (1, 2)), jnp.tile(sin_c, (1, 2))


def _lambda_init(layer_idx):
    return 0.8 - 0.6 * math.exp(-0.3 * layer_idx)


def _trunk(x, w):
    batch, seq, _ = x.shape
    x2d = x.reshape(batch * seq, D_MODEL)
    cos_t, sin_t = _rope_tables(seq)
    for i in range(DEPTH):
        kind, slot = i % N_MIXERS, i // N_MIXERS
        g0 = w["norm_gain"][i, 0:1]
        gains = w["norm_gain"][i, 1:4]
        wgu, wd = w["w_ffn_gate_up"][i], w["w_ffn_down"][i]
        if kind == 0:
            q, k, vt = _attn_proj(x2d, g0, w["wq"][slot], w["wk"][slot], w["wvt"][slot],
                                  cos_t, sin_t, seq)
            a = _diff_attn(q, k, vt, w["attn_lambda"][slot], w["subln_col"][slot],
                           batch, seq, _lambda_init(i))
            x2d = _attn_post(x2d, a, w["w_attn_out"][slot], gains, wgu, wd)
        elif kind == 1:
            x2d = _pool_post(x2d, g0, w["w_pool"][slot], w["b_pool"][slot],
                             w["pool_scale"][slot], gains, wgu, wd, seq)
        else:
            qt, k, vt, o, gr, uc = _mlstm_proj(
                x2d, g0, w["ml_wqt"][slot], w["ml_wk"][slot], w["ml_wvt"][slot],
                w["ml_wo"][slot], w["ml_wgt"][slot], w["ml_bgt"][slot])
            hf, hb = _mlstm_scan(qt, k, vt, gr, uc, batch, seq)
            x2d = _mlstm_post(x2d, hf.reshape(-1, D_MODEL), hb.reshape(-1, D_MODEL), o,
                              w["ml_head_gain"][slot], w["w_mlstm_out"][slot], gains, wgu, wd)
    return x2d.reshape(batch, seq, D_MODEL)


def _prepare_weights(norm_gain, w_attn_in, attn_lambda, attn_subln_gain, w_attn_out, w_pool,
                     b_pool, pool_scale, w_mlstm_in, b_mlstm_gate, mlstm_head_gain,
                     w_mlstm_out, w_ffn_gate_up, w_ffn_down):
    d, qk = D_MODEL, ML_QK_COLS
    swap = lambda a: jnp.swapaxes(a, -1, -2)
    wm = w_mlstm_in
    wg = wm[:, :, 2 * qk + 2 * d:]
    nh = ML_HEADS
    gate_order = jnp.array([kind * nh + h for kind in (1, 3, 0, 2) for h in range(nh)])
    return {
        "norm_gain": norm_gain,
        "wq": w_attn_in[:, :, :d].astype(BF16),
        "wk": w_attn_in[:, :, d:2 * d].astype(BF16),
        "wvt": swap(w_attn_in[:, :, 2 * d:]).astype(BF16),
        "attn_lambda": attn_lambda,
        "subln_col": attn_subln_gain[:, :, None],
        "w_attn_out": w_attn_out.astype(BF16),
        "w_pool": w_pool.astype(BF16),
        "b_pool": b_pool[:, None, :],
        "pool_scale": pool_scale[:, None, :],
        "ml_wqt": swap(wm[:, :, :qk]).astype(BF16),
        "ml_wk": wm[:, :, qk:2 * qk].astype(BF16),
        "ml_wvt": swap(wm[:, :, 2 * qk:2 * qk + d]).astype(BF16),
        "ml_wo": wm[:, :, 2 * qk + d:2 * qk + 2 * d].astype(BF16),
        "ml_wgt": swap(wg[:, :, gate_order]).astype(BF16),
        "ml_bgt": b_mlstm_gate[:, gate_order, None],
        "ml_head_gain": mlstm_head_gain[:, None, :],
        "w_mlstm_out": w_mlstm_out.astype(BF16),
        "w_ffn_gate_up": w_ffn_gate_up.astype(BF16),
        "w_ffn_down": w_ffn_down.astype(BF16),
    }


def kernel(x_prompt, x_sample, norm_gain, w_attn_in, attn_lambda, attn_subln_gain, w_attn_out,
           w_pool, b_pool, pool_scale, w_mlstm_in, b_mlstm_gate, mlstm_head_gain, w_mlstm_out,
           w_ffn_gate_up, w_ffn_down):
    w = _prepare_weights(norm_gain, w_attn_in, attn_lambda, attn_subln_gain, w_attn_out,
                         w_pool, b_pool, pool_scale, w_mlstm_in, b_mlstm_gate,
                         mlstm_head_gain, w_mlstm_out, w_ffn_gate_up, w_ffn_down)
    return _trunk(x_prompt, w), _trunk(x_sample, w)
```
